```python
import math
import jax
import jax.numpy as jnp
from jax import lax
import numpy as np

D_MODEL = 1024
BATCH = 16
SEQ = 2048
DEPTH = 2

DIFF_HEADS = 4
DIFF_QK_DIM = 64
DIFF_V_DIM = 128
GLA_HEADS = 4
GLA_DK = 64
GLA_DV = 128
GLA_GATE_RANK = 16
GLA_GATE_NORM = 16.0
HGRN_HEADS = 4
HGRN_EXPAND = 128
HGRN_DV = 128
N_BRANCHES = 3
BRANCH_WIDTH = 512
D_FF = 4 * D_MODEL
CHUNK = 64
Q_BLOCK = 128
ALPHA = (2 * DEPTH) ** 0.25
BETA = (8 * DEPTH) ** -0.25
LN_EPS = 1e-5
MASK_VALUE = -1e30
LB_FLOOR = 1e-30
SPLIT_SIZES = (
    DIFF_HEADS * 2 * DIFF_QK_DIM,
    DIFF_HEADS * 2 * DIFF_QK_DIM,
    DIFF_HEADS * DIFF_V_DIM,
    GLA_HEADS * GLA_DK,
    GLA_HEADS * GLA_DK,
    GLA_HEADS * GLA_DV,
    GLA_GATE_RANK,
    GLA_HEADS * GLA_DV,
    HGRN_HEADS * HGRN_EXPAND,
    HGRN_HEADS * HGRN_EXPAND,
    HGRN_HEADS * HGRN_DV,
    HGRN_HEADS * HGRN_DV,
    N_BRANCHES * D_MODEL,
)
IN_WIDTH = sum(SPLIT_SIZES)

kernel_name = 'hybrid_diffattn_gla_hgrn2_deepnorm'


def layer_norm(x, w, b):
    xf = x.astype(jnp.float32)
    mu = jnp.mean(xf, axis=-1, keepdims=True)
    var = jnp.mean(jnp.square(xf - mu), axis=-1, keepdims=True)
    y = (xf - mu) * lax.rsqrt(var + LN_EPS) * w.astype(jnp.float32) + b.astype(jnp.float32)
    return y.astype(x.dtype)


def rms_norm(x, w):
    xf = x.astype(jnp.float32)
    y = xf * lax.rsqrt(jnp.mean(jnp.square(xf), axis=-1, keepdims=True) + LN_EPS)
    return y * w.astype(jnp.float32)


def split_columns(z):
    points = np.cumsum(np.array(SPLIT_SIZES))[:-1]
    return jnp.split(z, points, axis=-1)


def to_heads(a, n_heads):
    b, t, _ = a.shape
    return a.reshape(b, t, n_heads, -1).transpose(0, 2, 1, 3)


def from_heads(a):
    b, h, t, d = a.shape
    return a.transpose(0, 2, 1, 3).reshape(b, t, h * d)


def diff_attention(q, k, v, lam):
    h = q.shape[1]
    t = q.shape[3]
    d = q.shape[4]
    slopes = jnp.exp2(-8.0 * jnp.arange(1, h + 1, dtype=jnp.float32) / h)
    scale = d ** -0.5
    vf = v.astype(jnp.float32)
    outs = []
    for blk in range(t // Q_BLOCK):
        start = blk * Q_BLOCK
        end = start + Q_BLOCK
        qb = q[:, :, :, start:end].astype(jnp.float32)
        kb = k[:, :, :, :end].astype(jnp.float32)
        s = jnp.einsum('bhmqd,bhmkd->bhmqk', qb, kb) * scale
        dist = (jnp.arange(start, end)[:, None] - jnp.arange(end)[None, :]).astype(jnp.float32)
        s = s - slopes[:, None, None, None] * dist
        s = jnp.where(dist >= 0, s, MASK_VALUE)
        p = jax.nn.softmax(s, axis=-1)
        a = p[:, :, 0] - lam * p[:, :, 1]
        outs.append(jnp.einsum('bhqk,bhkd->bhqd', a, vf[:, :, :end]))
    return jnp.concatenate(outs, axis=2)


def chunked_gated_linear_attention(q, k, v, log_g):
    b, h, t, dk = q.shape
    dv = v.shape[-1]
    nc = t // CHUNK

    def to_chunks(a):
        return a.astype(jnp.float32).reshape(b, h, nc, CHUNK, a.shape[-1]).transpose(2, 0, 1, 3, 4)

    qc, kc, vc, gc = to_chunks(q), to_chunks(k), to_chunks(v), to_chunks(log_g)
    causal = jnp.tril(jnp.ones((CHUNK, CHUNK), dtype=bool))[:, :, None]

    def step(state, inp):
        qi, ki, vi, gi = inp
        cum = jnp.cumsum(gi, axis=-2)
        diff = cum[..., :, None, :] - cum[..., None, :, :]
        decay = jnp.where(causal, jnp.exp(jnp.where(causal, diff, 0.0)), 0.0)
        scores = jnp.einsum('bhid,bhjd,bhijd->bhij', qi, ki, decay)
        o = jnp.einsum('bhij,bhje->bhie', scores, vi)
        o = o + jnp.einsum('bhid,bhde->bhie', qi * jnp.exp(cum), state)
        last = cum[..., -1:, :]
        new_state = jnp.exp(last)[..., 0, :, None] * state + jnp.einsum(
            'bhjd,bhje->bhde', ki * jnp.exp(last - cum), vi)
        return new_state, o

    s0 = jnp.zeros((b, h, dk, dv), dtype=jnp.float32)
    _, o = lax.scan(step, s0, (qc, kc, vc, gc))
    return o.transpose(1, 2, 0, 3, 4).reshape(b, h, t, dv)


def setup_inputs(seed: int = 0) -> dict:
    key = jax.random.key(seed)
    ks = jax.random.split(key, 18)
    n = jax.random.normal
    f32 = jnp.float32
    return {
        'x': n(ks[0], (BATCH, SEQ, D_MODEL), f32),
        'w_in': n(ks[1], (DEPTH, D_MODEL, IN_WIDTH), f32) * D_MODEL ** -0.5,
        'gla_w_gate': n(ks[2], (DEPTH, GLA_GATE_RANK, GLA_HEADS * GLA_DK), f32) * GLA_GATE_RANK ** -0.5,
        'gla_b_gate': 0.01 * n(ks[3], (DEPTH, GLA_HEADS * GLA_DK), f32),
        'diff_lambda': 0.1 * n(ks[4], (DEPTH, 4, DIFF_QK_DIM), f32),
        'diff_norm_w': 1.0 + 0.02 * n(ks[5], (DEPTH, DIFF_V_DIM), f32),
        'gla_norm_w': 1.0 + 0.02 * n(ks[6], (DEPTH, GLA_DV), f32),
        'hgrn_norm_w': 1.0 + 0.02 * n(ks[7], (DEPTH, HGRN_DV), f32),
        'hgrn_lb': n(ks[8], (DEPTH, HGRN_HEADS * HGRN_EXPAND), f32),
        'w_branch': n(ks[9], (DEPTH, N_BRANCHES, BRANCH_WIDTH, D_MODEL), f32) * BRANCH_WIDTH ** -0.5 * BETA,
        'w_out': n(ks[10], (DEPTH, D_MODEL, D_MODEL), f32) * D_MODEL ** -0.5 * BETA,
        'ln1_w': 1.0 + 0.02 * n(ks[11], (DEPTH, D_MODEL), f32),
        'ln1_b': 0.02 * n(ks[12], (DEPTH, D_MODEL), f32),
        'w_up': n(ks[13], (DEPTH, D_MODEL, D_FF), f32) * D_MODEL ** -0.5 * BETA,
        'w_down': n(ks[14], (DEPTH, D_FF, D_MODEL), f32) * D_FF ** -0.5 * BETA,
        'ln2_w': 1.0 + 0.02 * n(ks[15], (DEPTH, D_MODEL), f32),
        'ln2_b': 0.02 * n(ks[16], (DEPTH, D_MODEL), f32),
    }


def reference(x, w_in, gla_w_gate, gla_b_gate, diff_lambda, diff_norm_w, gla_norm_w,
              hgrn_norm_w, hgrn_lb, w_branch, w_out, ln1_w, ln1_b, w_up, w_down, ln2_w, ln2_b):
    dt = x.dtype
    b, t, _ = x.shape
    lb_soft = jax.nn.softmax(hgrn_lb.astype(jnp.float32), axis=0)
    lb_all = jnp.cumsum(lb_soft, axis=0) - lb_soft[0:1]
    for l in range(DEPTH):
        z = x @ w_in[l]
        (qa, ka, va, qg, kg, vg, glr, gout, qh, fh, ih, hout, mg) = split_columns(z)

        qa = qa.reshape(b, t, DIFF_HEADS, 2, DIFF_QK_DIM).transpose(0, 2, 3, 1, 4)
        ka = ka.reshape(b, t, DIFF_HEADS, 2, DIFF_QK_DIM).transpose(0, 2, 3, 1, 4)
        va = to_heads(va, DIFF_HEADS)
        lam_init = 0.8 - 0.6 * math.exp(-0.3 * l)
        lp = diff_lambda[l].astype(jnp.float32)
        lam = jnp.exp(jnp.sum(lp[0] * lp[1])) - jnp.exp(jnp.sum(lp[2] * lp[3])) + lam_init
        oa = diff_attention(qa, ka, va, lam)
        oa = from_heads(rms_norm(oa, diff_norm_w[l]) * (1.0 - lam_init))

        log_g = jax.nn.log_sigmoid((glr @ gla_w_gate[l] + gla_b_gate[l]).astype(jnp.float32)) / GLA_GATE_NORM
        ob = chunked_gated_linear_attention(
            to_heads(qg, GLA_HEADS) * GLA_DK ** -0.5, to_heads(kg, GLA_HEADS),
            to_heads(vg, GLA_HEADS), to_heads(log_g, GLA_HEADS))
        ob = from_heads(rms_norm(ob, gla_norm_w[l]) * jax.nn.silu(to_heads(gout, GLA_HEADS).astype(jnp.float32)))

        lb = lb_all[l]
        log_f = jnp.logaddexp(jnp.log(jnp.maximum(lb, LB_FLOOR)),
                              jnp.log1p(-lb) + jax.nn.log_sigmoid(fh.astype(jnp.float32)))
        k_h = -jnp.expm1(log_f)
        oc = chunked_gated_linear_attention(
            to_heads(qh, HGRN_HEADS), to_heads(k_h, HGRN_HEADS),
            to_heads(ih, HGRN_HEADS), to_heads(log_f, HGRN_HEADS))
        oc = from_heads(rms_norm(oc, hgrn_norm_w[l]) * jax.nn.silu(to_heads(hout, HGRN_HEADS).astype(jnp.float32)))

        branches = jnp.stack([oa, ob, oc], axis=2).astype(dt)
        y = jnp.einsum('btnc,ncd->btnd', branches, w_branch[l])
        gates = jax.nn.sigmoid(mg.reshape(b, t, N_BRANCHES, D_MODEL))
        mix = jnp.einsum('btnd,btnd->btd', gates, y) @ w_out[l]
        x = layer_norm(ALPHA * x + mix, ln1_w[l], ln1_b[l])

        hmid = jnp.square(jax.nn.relu(x @ w_up[l]))
        x = layer_norm(ALPHA * x + hmid @ w_down[l], ln2_w[l], ln2_b[l])
    return x
```

```python
import functools
import math

import numpy as np
import jax
import jax.numpy as jnp
from jax import lax
from jax.experimental import pallas as pl
from jax.experimental.pallas import tpu as pltpu

D_MODEL = 1024
DEPTH = 2
DIFF_HEADS = 4
DIFF_QK_DIM = 64
DIFF_V_DIM = 128
GLA_HEADS = 4
GLA_DK = 64
GLA_DV = 128
GLA_GATE_RANK = 16
GLA_GATE_NORM = 16.0
HGRN_HEADS = 4
HGRN_EXPAND = 128
HGRN_DV = 128
N_BRANCHES = 3
BRANCH_WIDTH = 512
D_FF = 4 * D_MODEL
ALPHA = (2 * DEPTH) ** 0.25
LN_EPS = 1e-5
MASK_VALUE = -1e30
LB_FLOOR = 1e-30

LANES = 128
SUBLANES = 8
VMEM_LIMIT_BYTES = 56 * 1024 * 1024

ROW_TILE = 512
ATTN_TILE = 256
CHUNK = 128
REC_ROWS = 256

F32 = jnp.float32
BF16 = jnp.bfloat16

_A_Q, _A_K, _A_V = 0, 512, 1024
_B_Q, _B_K, _B_V, _B_R, _B_G = 1536, 1792, 2048, 2560, 2576
_C_Q, _C_F, _C_I, _C_G = 3088, 3600, 4112, 4624
_MG = 5136
IN_WIDTH = _MG + N_BRANCHES * D_MODEL


def _dot(a, b):
    return jnp.dot(a, b, preferred_element_type=F32)


def _dot_nt(a, b):
    return lax.dot_general(a, b, (((1,), (1,)), ((), ())), preferred_element_type=F32)


def _dot_tn(a, b):
    return lax.dot_general(a, b, (((0,), (0,)), ((), ())), preferred_element_type=F32)


def _split_bf16(a):
    hi = a.astype(BF16)
    r = a - hi.astype(F32)
    mid = r.astype(BF16)
    lo = (r - mid.astype(F32)).astype(BF16)
    return hi, mid, lo


def _log_sigmoid(u):
    return jnp.minimum(u, 0.0) - jnp.log1p(jnp.exp(-jnp.abs(u)))


def _sigmoid(u):
    return 1.0 / (1.0 + jnp.exp(-u))


def _layer_norm(r, w, b):
    mu = jnp.mean(r, axis=-1, keepdims=True)
    c = r - mu
    var = jnp.mean(c * c, axis=-1, keepdims=True)
    return c * lax.rsqrt(var + LN_EPS) * w + b


def _rms_norm(o, w):
    return o * lax.rsqrt(jnp.mean(o * o, axis=-1, keepdims=True) + LN_EPS) * w


def _const_spec(shape):
    nd = len(shape)
    return pl.BlockSpec(shape, lambda *_: (0,) * nd, pipeline_mode=pl.Buffered(1))


def _params(semantics):
    return pltpu.CompilerParams(dimension_semantics=semantics, vmem_limit_bytes=VMEM_LIMIT_BYTES)


def _in_proj_kernel(x_ref, wa_ref, wb_ref, wr_ref, wc_ref, gw_ref, gb_ref, lb_ref,
                    qa_ref, ka_ref, va_ref, qg_ref, kg_ref, vg_ref, lg_ref, sgo_ref,
                    qh_ref, lf_ref, kh_ref, ih_ref, sho_ref, *, layer):
    xb = x_ref[...].astype(BF16)

    def proj(w_ref, lo, width):
        return _dot(xb, w_ref[:, lo:lo + width])

    qa_ref[...] = (proj(wa_ref, 0, 512) * DIFF_QK_DIM ** -0.5).astype(BF16)
    ka_ref[...] = proj(wa_ref, 512, 512).astype(BF16)
    va_ref[...] = proj(wa_ref, 1024, 512).astype(BF16)

    qg_ref[...] = (proj(wb_ref, 0, 256) * GLA_DK ** -0.5).astype(BF16)
    kg_ref[...] = proj(wb_ref, 256, 256).astype(BF16)
    vg_ref[...] = proj(wb_ref, 512, 512).astype(BF16)
    gout = proj(wb_ref, 1024, 512)
    sgo_ref[...] = (gout * _sigmoid(gout)).astype(BF16)
    glr = proj(wr_ref, 0, LANES)
    a_hi, a_mid, _ = _split_bf16(glr)
    w_hi, w_mid, _ = _split_bf16(gw_ref[...])
    u = _dot(a_hi, w_hi) + (_dot(a_hi, w_mid) + _dot(a_mid, w_hi)) + gb_ref[...]
    lg_ref[...] = _log_sigmoid(u) * (1.0 / GLA_GATE_NORM)

    rows = [lb_ref[d:d + 1, :] for d in range(DEPTH)]
    mx = functools.reduce(jnp.maximum, rows)
    es = [jnp.exp(r - mx) for r in rows]
    tot = functools.reduce(jnp.add, es)
    soft = [e / tot for e in es]
    lb = functools.reduce(jnp.add, soft[:layer + 1]) - soft[0]
    log_lb = jnp.log(jnp.maximum(lb, LB_FLOOR))
    log_1m_lb = jnp.log1p(-lb)
    qh_ref[...] = proj(wc_ref, 0, 512).astype(BF16)
    b2 = log_1m_lb + _log_sigmoid(proj(wc_ref, 512, 512))
    log_f = jnp.maximum(log_lb, b2) + jnp.log1p(jnp.exp(-jnp.abs(log_lb - b2)))
    lf_ref[...] = log_f
    kh_ref[...] = (1.0 - jnp.exp(log_f)).astype(BF16)
    ih_ref[...] = proj(wc_ref, 1024, 512).astype(BF16)
    hout = proj(wc_ref, 1536, 512)
    sho_ref[...] = (hout * _sigmoid(hout)).astype(BF16)


def _in_proj(x2d, wa, wb, wr, wc, gw, gb, lb, layer):
    n = x2d.shape[0]
    tm = min(ROW_TILE, n)
    row = lambda width: pl.BlockSpec((tm, width), lambda i: (i, 0))
    out_widths = [(512, BF16), (512, BF16), (512, BF16),
                  (256, BF16), (256, BF16), (512, BF16), (256, F32), (512, BF16),
                  (512, BF16), (512, F32), (512, BF16), (512, BF16), (512, BF16)]
    return pl.pallas_call(
        functools.partial(_in_proj_kernel, layer=layer),
        grid=(n // tm,),
        in_specs=[row(D_MODEL), _const_spec(wa.shape), _const_spec(wb.shape), _const_spec(wr.shape),
                  _const_spec(wc.shape), _const_spec(gw.shape), _const_spec(gb.shape), _const_spec(lb.shape)],
        out_specs=[row(w) for w, _ in out_widths],
        out_shape=[jax.ShapeDtypeStruct((n, w), dt) for w, dt in out_widths],
        compiler_params=_params(("parallel",)),
        name="in_proj",
    )(x2d, wa, wb, wr, wc, gw, gb, lb)


def _diff_attn_kernel(q_ref, k_ref, v_ref, lam_ref, nw_ref, o_ref, m_scr, l_scr, acc_scr, *, layer):
    t = ATTN_TILE
    h = pl.program_id(1)
    qi = pl.program_id(2)

    q = q_ref[0]
    lane = lax.broadcasted_iota(jnp.int32, q.shape, 1)
    zero = jnp.zeros_like(q)
    qs = jnp.concatenate([jnp.where(lane < DIFF_QK_DIM, q, zero),
                          jnp.where(lane >= DIFF_QK_DIM, q, zero)], axis=0)

    slope = jnp.float32(0.0)
    for hh in range(DIFF_HEADS):
        slope = jnp.where(h == hh, jnp.float32(2.0 ** (-8.0 * (hh + 1) / DIFF_HEADS)), slope)
    col = lax.broadcasted_iota(jnp.int32, (1, t), 1)

    m_scr[...] = jnp.full(m_scr.shape, -jnp.inf, F32)
    l_scr[...] = jnp.zeros(l_scr.shape, F32)
    acc_scr[...] = jnp.zeros(acc_scr.shape, F32)

    def block(j, masked):
        kb = k_ref[0, pl.ds(pl.multiple_of(j * t, t), t), :]
        vb = v_ref[0, pl.ds(pl.multiple_of(j * t, t), t), :]
        bias = slope * ((j - qi) * t + col).astype(F32)
        s = _dot_nt(qs, kb) + bias
        if masked:
            r = lax.broadcasted_iota(jnp.int32, (t, t), 0)
            c = lax.broadcasted_iota(jnp.int32, (t, t), 1)
            keep = jnp.concatenate([r >= c, r >= c], axis=0)
            s = jnp.where(keep, s, MASK_VALUE)
        m_old = m_scr[...]
        m_new = jnp.maximum(m_old, jnp.max(s, axis=-1, keepdims=True))
        corr = jnp.exp(m_old - m_new)
        p = jnp.exp(s - m_new)
        l_scr[...] = l_scr[...] * corr + jnp.sum(p, axis=-1, keepdims=True)
        acc_scr[...] = acc_scr[...] * corr + _dot(p.astype(BF16), vb)
        m_scr[...] = m_new

    def body(j, carry):
        block(j, False)
        return carry

    lax.fori_loop(0, qi, body, 0)
    block(qi, True)

    lp = lam_ref[...]
    lam_init = 0.8 - 0.6 * math.exp(-0.3 * layer)
    lam = (jnp.exp(jnp.sum(lp[0:1] * lp[1:2], axis=-1, keepdims=True))
           - jnp.exp(jnp.sum(lp[2:3] * lp[3:4], axis=-1, keepdims=True)) + lam_init)
    o12 = acc_scr[...] / l_scr[...]
    o = o12[:t] - lam * o12[t:]
    o_ref[0] = (_rms_norm(o, nw_ref[...]) * (1.0 - lam_init)).astype(o_ref.dtype)


def _diff_attn(qa, ka, va, lam_p, norm_w, layer):
    b, t, _ = qa.shape
    tq = ATTN_TILE
    return pl.pallas_call(
        functools.partial(_diff_attn_kernel, layer=layer),
        grid=(b, DIFF_HEADS, t // tq),
        in_specs=[pl.BlockSpec((1, tq, LANES), lambda bi, h, qi: (bi, qi, h)),
                  pl.BlockSpec((1, t, LANES), lambda bi, h, qi: (bi, 0, h)),
                  pl.BlockSpec((1, t, LANES), lambda bi, h, qi: (bi, 0, h)),
                  _const_spec(lam_p.shape), _const_spec(norm_w.shape)],
        out_specs=pl.BlockSpec((1, tq, LANES), lambda bi, h, qi: (bi, qi, h)),
        out_shape=jax.ShapeDtypeStruct((b, t, DIFF_HEADS * DIFF_V_DIM), BF16),
        scratch_shapes=[pltpu.VMEM((2 * tq, 1), F32), pltpu.VMEM((2 * tq, 1), F32),
                        pltpu.VMEM((2 * tq, DIFF_V_DIM), F32)],
        compiler_params=_params(("parallel", "parallel", "arbitrary")),
        name="diff_attn",
    )(qa, ka, va, lam_p, norm_w)


def _level_table(n):
    i = np.arange(n)[:, None]
    j = np.arange(n)[None, :]
    x = np.bitwise_xor(i, j)
    lvl = np.where(i > j, np.floor(np.log2(np.maximum(x, 1))), -1.0)
    return lvl.astype(np.int32)


def _boundary_rows(cum_ref, s):
    c = CHUNK
    if s >= SUBLANES:
        parts = [jnp.broadcast_to(cum_ref[pl.ds(g0 + s - 1, 1), :], (2 * s, LANES))
                 for g0 in range(0, c, 2 * s)]
    elif s == 4:
        parts = [jnp.broadcast_to(cum_ref[pl.ds(g0 + 3, 1), :], (SUBLANES, LANES))
                 for g0 in range(0, c, SUBLANES)]
    else:
        sub = lax.broadcasted_iota(jnp.int32, (SUBLANES, LANES), 0)
        parts = [jnp.where(sub < 4,
                           jnp.broadcast_to(cum_ref[pl.ds(g0 + 1, 1), :], (SUBLANES, LANES)),
                           jnp.broadcast_to(cum_ref[pl.ds(g0 + 5, 1), :], (SUBLANES, LANES)))
                 for g0 in range(0, c, SUBLANES)]
    return jnp.concatenate(parts, axis=0) if len(parts) > 1 else parts[0]


def _gla_kernel(q_ref, k_ref, v_ref, g_ref, og_ref, nw_ref, tri_ref, lvl_ref, o_ref,
                st_scr, cum_scr, *, dk, n_heads):
    heads_per_block = LANES // dk
    n_blocks = n_heads // heads_per_block
    c = CHUNK
    n_levels = int(math.log2(c))

    @pl.when(pl.program_id(1) == 0)
    def _():
        st_scr[...] = jnp.zeros(st_scr.shape, F32)

    tri = tri_ref[...]
    lvl = lvl_ref[...]
    nw = nw_ref[...]
    lane = lax.broadcasted_iota(jnp.int32, (c, LANES), 1)

    for ci in range(REC_ROWS // c):
        r0 = ci * c
        for blk in range(n_blocks):
            l0 = blk * LANES
            g = g_ref[0, r0:r0 + c, l0:l0 + LANES]
            qf = q_ref[0, r0:r0 + c, l0:l0 + LANES].astype(F32)
            kf = k_ref[0, r0:r0 + c, l0:l0 + LANES].astype(F32)
            g_hi, g_mid, g_lo = _split_bf16(g)
            cum = _dot(tri, g_hi) + (_dot(tri, g_mid) + _dot(tri, g_lo))
            cum_scr[...] = cum
            cum_last = cum_scr[pl.ds(c - 1, 1), :]

            q_lv = [(qf * jnp.exp(g)).astype(BF16)]
            k_lv = [kf.astype(BF16)]
            for li in range(1, n_levels):
                d = cum - _boundary_rows(cum_scr, 1 << li)
                q_lv.append((qf * jnp.exp(jnp.minimum(d, 0.0))).astype(BF16))
                k_lv.append((kf * jnp.exp(jnp.minimum(-d, 0.0))).astype(BF16))
            q_in = (qf * jnp.exp(cum)).astype(BF16)
            k_out = (kf * jnp.exp(cum_last - cum)).astype(BF16)
            qk = qf * kf

            for hb in range(heads_per_block):
                h = blk * heads_per_block + hb
                in_head = (lane >= hb * dk) & (lane < (hb + 1) * dk)
                sel = (lambda a: a) if heads_per_block == 1 else (
                    lambda a: jnp.where(in_head, a, jnp.zeros_like(a)))
                vh = v_ref[0, r0:r0 + c, h * LANES:(h + 1) * LANES]
                scores = jnp.zeros((c, c), F32)
                for li in range(n_levels):
                    scores = jnp.where(lvl == li, _dot_nt(sel(q_lv[li]), k_lv[li]), scores)
                diag = jnp.sum(sel(qk), axis=-1, keepdims=True)
                st = st_scr[h]
                o = (_dot(scores.astype(BF16), vh) + diag * vh.astype(F32)
                     + _dot_nt(sel(q_in), st.astype(BF16)))
                st_scr[h] = st * jnp.exp(cum_last) + _dot_tn(vh, k_out)
                og = og_ref[0, r0:r0 + c, h * LANES:(h + 1) * LANES].astype(F32)
                o_ref[0, r0:r0 + c, h * LANES:(h + 1) * LANES] = (_rms_norm(o, nw) * og).astype(o_ref.dtype)


def _gated_linear_attention(q, k, v, g, og, norm_w, dk, n_heads, name):
    b, t, hk = q.shape
    hv = v.shape[-1]
    rows = REC_ROWS
    tri = jnp.asarray(np.tril(np.ones((CHUNK, CHUNK), np.float32)), BF16)
    lvl = jnp.asarray(_level_table(CHUNK))
    seq = lambda width: pl.BlockSpec((1, rows, width), lambda bi, ci: (bi, ci, 0))
    return pl.pallas_call(
        functools.partial(_gla_kernel, dk=dk, n_heads=n_heads),
        grid=(b, t // rows),
        in_specs=[seq(hk), seq(hk), seq(hv), seq(hk), seq(hv),
                  _const_spec(norm_w.shape), _const_spec(tri.shape), _const_spec(lvl.shape)],
        out_specs=seq(hv),
        out_shape=jax.ShapeDtypeStruct((b, t, hv), BF16),
        scratch_shapes=[pltpu.VMEM((n_heads, LANES, LANES), F32), pltpu.VMEM((CHUNK, LANES), F32)],
        compiler_params=_params(("parallel", "arbitrary")),
        name=name,
    )(q, k, v, g, og, norm_w, tri, lvl)


def _merge_kernel(x_ref, oa_ref, ob_ref, oc_ref, wmg_ref, wbr_ref, wout_ref, lnw_ref, lnb_ref, o_ref):
    x = x_ref[...]
    xb = x.astype(BF16)
    mixed = None
    for n, br_ref in enumerate((oa_ref, ob_ref, oc_ref)):
        gate = _sigmoid(_dot(xb, wmg_ref[:, n * D_MODEL:(n + 1) * D_MODEL]))
        y = gate * _dot(br_ref[...], wbr_ref[n])
        mixed = y if mixed is None else mixed + y
    mix = _dot(mixed.astype(BF16), wout_ref[...])
    o_ref[...] = _layer_norm(ALPHA * x + mix, lnw_ref[...], lnb_ref[...])


def _merge(x2d, oa, ob, oc, wmg, wbr, wout, lnw, lnb):
    n = x2d.shape[0]
    tm = min(ROW_TILE, n)
    row = lambda width: pl.BlockSpec((tm, width), lambda i: (i, 0))
    return pl.pallas_call(
        _merge_kernel,
        grid=(n // tm,),
        in_specs=[row(D_MODEL), row(BRANCH_WIDTH), row(BRANCH_WIDTH), row(BRANCH_WIDTH),
                  _const_spec(wmg.shape), _const_spec(wbr.shape), _const_spec(wout.shape),
                  _const_spec(lnw.shape), _const_spec(lnb.shape)],
        out_specs=row(D_MODEL),
        out_shape=jax.ShapeDtypeStruct((n, D_MODEL), F32),
        compiler_params=_params(("parallel",)),
        name="merge",
    )(x2d, oa, ob, oc, wmg, wbr, wout, lnw, lnb)


def _mlp_kernel(x_ref, wup_ref, wdn_ref, lnw_ref, lnb_ref, o_ref):
    x = x_ref[...]
    xb = x.astype(BF16)
    acc = None
    for f0 in range(0, D_FF, D_MODEL):
        hmid = jnp.maximum(_dot(xb, wup_ref[:, f0:f0 + D_MODEL]), 0.0)
        part = _dot((hmid * hmid).astype(BF16), wdn_ref[f0:f0 + D_MODEL, :])
        acc = part if acc is None else acc + part
    o_ref[...] = _layer_norm(ALPHA * x + acc, lnw_ref[...], lnb_ref[...])


def _mlp(x2d, wup, wdn, lnw, lnb):
    n = x2d.shape[0]
    tm = min(ROW_TILE, n)
    row = pl.BlockSpec((tm, D_MODEL), lambda i: (i, 0))
    return pl.pallas_call(
        _mlp_kernel,
        grid=(n // tm,),
        in_specs=[row, _const_spec(wup.shape), _const_spec(wdn.shape),
                  _const_spec(lnw.shape), _const_spec(lnb.shape)],
        out_specs=row,
        out_shape=jax.ShapeDtypeStruct((n, D_MODEL), F32),
        compiler_params=_params(("parallel",)),
        name="mlp",
    )(x2d, wup, wdn, lnw, lnb)


def _layer(x2d, b, t, layer, w_in, gla_w_gate, gla_b_gate, diff_lambda, diff_norm_w, gla_norm_w,
           hgrn_norm_w, hgrn_lb, w_branch, w_out, ln1_w, ln1_b, w_up, w_down, ln2_w, ln2_b):
    n = b * t
    w = w_in[layer]
    wa = w[:, _A_Q:_B_Q].astype(BF16)
    wb = jnp.concatenate([w[:, _B_Q:_B_R], w[:, _B_G:_C_Q]], axis=1).astype(BF16)
    wr = jnp.pad(w[:, _B_R:_B_G], ((0, 0), (0, LANES - GLA_GATE_RANK))).astype(BF16)
    wc = w[:, _C_Q:_MG].astype(BF16)
    wmg = w[:, _MG:].astype(BF16)
    gw = jnp.pad(gla_w_gate[layer], ((0, LANES - GLA_GATE_RANK), (0, 0)))
    gb = gla_b_gate[layer][None, :]

    (qa, ka, va, qg, kg, vg, lg, sgo, qh, lf, kh, ih, sho) = _in_proj(
        x2d, wa, wb, wr, wc, gw, gb, hgrn_lb, layer)
    seq = lambda a: a.reshape(b, t, a.shape[-1])

    oa = _diff_attn(seq(qa), seq(ka), seq(va), diff_lambda[layer], diff_norm_w[layer][None, :], layer)
    ob = _gated_linear_attention(seq(qg), seq(kg), seq(vg), seq(lg), seq(sgo),
                                 gla_norm_w[layer][None, :], GLA_DK, GLA_HEADS, "gla")
    oc = _gated_linear_attention(seq(qh), seq(kh), seq(ih), seq(lf), seq(sho),
                                 hgrn_norm_w[layer][None, :], HGRN_EXPAND, HGRN_HEADS, "hgrn")

    flat = lambda a: a.reshape(n, a.shape[-1])
    x1 = _merge(x2d, flat(oa), flat(ob), flat(oc), wmg, w_branch[layer].astype(BF16),
                w_out[layer].astype(BF16), ln1_w[layer][None, :], ln1_b[layer][None, :])
    return _mlp(x1, w_up[layer].astype(BF16), w_down[layer].astype(BF16),
                ln2_w[layer][None, :], ln2_b[layer][None, :])


def kernel(x, w_in, gla_w_gate, gla_b_gate, diff_lambda, diff_norm_w, gla_norm_w, hgrn_norm_w, hgrn_lb,
           w_branch, w_out, ln1_w, ln1_b, w_up, w_down, ln2_w, ln2_b):
    b, t, d = x.shape
    assert d == D_MODEL and w_in.shape == (DEPTH, D_MODEL, IN_WIDTH)
    assert t % REC_ROWS == 0 and t % ATTN_TILE == 0 and (b * t) % min(ROW_TILE, b * t) == 0
    x2d = x.reshape(b * t, d)
    for layer in range(DEPTH):
        x2d = _layer(x2d, b, t, layer, w_in, gla_w_gate, gla_b_gate, diff_lambda, diff_norm_w,
                     gla_norm_w, hgrn_norm_w, hgrn_lb, w_branch, w_out, ln1_w, ln1_b,
                     w_up, w_down, ln2_w, ln2_b)
    return x2d.reshape(b, t, d)
```

```python
import functools
import math

import numpy as np
import jax
import jax.numpy as jnp
from jax import lax
from jax.experimental import pallas as pl
from jax.experimental.pallas import tpu as pltpu

D_MODEL = 1024
DEPTH = 2
DIFF_HEADS = 4
DIFF_QK_DIM = 64
DIFF_V_DIM = 128
GLA_HEADS = 4
GLA_DK = 64
GLA_DV = 128
GLA_GATE_RANK = 16
GLA_GATE_NORM = 16.0
HGRN_HEADS = 4
HGRN_EXPAND = 128
HGRN_DV = 128
N_BRANCHES = 3
BRANCH_WIDTH = 512
D_FF = 4 * D_MODEL
ALPHA = (2 * DEPTH) ** 0.25
LN_EPS = 1e-5
MASK_VALUE = -1e30
LB_FLOOR = 1e-30

LANES = 128
SUBLANES = 8
VMEM_LIMIT_BYTES = 56 * 1024 * 1024

ROW_TILE = 512
ATTN_TILE = 256
CHUNK = 128
REC_ROWS = 256

F32 = jnp.float32
BF16 = jnp.bfloat16

_A_Q, _A_K, _A_V = 0, 512, 1024
_B_Q, _B_K, _B_V, _B_R, _B_G = 1536, 1792, 2048, 2560, 2576
_C_Q, _C_F, _C_I, _C_G = 3088, 3600, 4112, 4624
_MG = 5136
IN_WIDTH = _MG + N_BRANCHES * D_MODEL


def _dot(a, b):
    return jnp.dot(a, b, preferred_element_type=F32)


def _dot_nt(a, b):
    return lax.dot_general(a, b, (((1,), (1,)), ((), ())), preferred_element_type=F32)


def _dot_tn(a, b):
    return lax.dot_general(a, b, (((0,), (0,)), ((), ())), preferred_element_type=F32)


def _split_bf16(a):
    hi = a.astype(BF16)
    r = a - hi.astype(F32)
    mid = r.astype(BF16)
    lo = (r - mid.astype(F32)).astype(BF16)
    return hi, mid, lo


def _log_sigmoid(u):
    return jnp.minimum(u, 0.0) - jnp.log1p(jnp.exp(-jnp.abs(u)))


def _sigmoid(u):
    return 1.0 / (1.0 + jnp.exp(-u))


def _layer_norm(r, w, b):
    mu = jnp.mean(r, axis=-1, keepdims=True)
    c = r - mu
    var = jnp.mean(c * c, axis=-1, keepdims=True)
    return c * lax.rsqrt(var + LN_EPS) * w + b


def _rms_norm(o, w):
    return o * lax.rsqrt(jnp.mean(o * o, axis=-1, keepdims=True) + LN_EPS) * w


def _const_spec(shape):
    nd = len(shape)
    return pl.BlockSpec(shape, lambda *_: (0,) * nd, pipeline_mode=pl.Buffered(1))


def _params(semantics):
    return pltpu.CompilerParams(dimension_semantics=semantics, vmem_limit_bytes=VMEM_LIMIT_BYTES)


def _in_proj_kernel(x_ref, wa_ref, wb_ref, wr_ref, wc_ref, gw_ref, gb_ref, lb_ref,
                    qa_ref, ka_ref, va_ref, qg_ref, kg_ref, vg_ref, lg_ref, sgo_ref,
                    qh_ref, lf_ref, kh_ref, ih_ref, sho_ref, *, layer):
    xb = x_ref[...].astype(BF16)

    def proj(w_ref, lo, width):
        return _dot(xb, w_ref[:, lo:lo + width])

    qa_ref[...] = (proj(wa_ref, 0, 512) * DIFF_QK_DIM ** -0.5).astype(BF16)
    ka_ref[...] = proj(wa_ref, 512, 512).astype(BF16)
    va_ref[...] = proj(wa_ref, 1024, 512).astype(BF16)

    qg_ref[...] = (proj(wb_ref, 0, 256) * GLA_DK ** -0.5).astype(BF16)
    kg_ref[...] = proj(wb_ref, 256, 256).astype(BF16)
    vg_ref[...] = proj(wb_ref, 512, 512).astype(BF16)
    gout = proj(wb_ref, 1024, 512)
    sgo_ref[...] = (gout * _sigmoid(gout)).astype(BF16)
    glr = proj(wr_ref, 0, LANES)
    a_hi, a_mid, _ = _split_bf16(glr)
    w_hi, w_mid, _ = _split_bf16(gw_ref[...])
    u = _dot(a_hi, w_hi) + (_dot(a_hi, w_mid) + _dot(a_mid, w_hi)) + gb_ref[...]
    lg_ref[...] = _log_sigmoid(u) * (1.0 / GLA_GATE_NORM)

    rows = [lb_ref[d:d + 1, :] for d in range(DEPTH)]
    mx = functools.reduce(jnp.maximum, rows)
    es = [jnp.exp(r - mx) for r in rows]
    tot = functools.reduce(jnp.add, es)
    soft = [e / tot for e in es]
    lb = functools.reduce(jnp.add, soft[:layer + 1]) - soft[0]
    log_lb = jnp.log(jnp.maximum(lb, LB_FLOOR))
    log_1m_lb = jnp.log1p(-lb)
    qh_ref[...] = proj(wc_ref, 0, 512).astype(BF16)
    b2 = log_1m_lb + _log_sigmoid(proj(wc_ref, 512, 512))
    log_f = jnp.maximum(log_lb, b2) + jnp.log1p(jnp.exp(-jnp.abs(log_lb - b2)))
    lf_ref[...] = log_f
    kh_ref[...] = (1.0 - jnp.exp(log_f)).astype(BF16)
    ih_ref[...] = proj(wc_ref, 1024, 512).astype(BF16)
    hout = proj(wc_ref, 1536, 512)
    sho_ref[...] = (hout * _sigmoid(hout)).astype(BF16)


def _in_proj(x2d, wa, wb, wr, wc, gw, gb, lb, layer):
    n = x2d.shape[0]
    tm = min(ROW_TILE, n)
    row = lambda width: pl.BlockSpec((tm, width), lambda i: (i, 0))
    out_widths = [(512, BF16), (512, BF16), (512, BF16),
                  (256, BF16), (256, BF16), (512, BF16), (256, F32), (512, BF16),
                  (512, BF16), (512, F32), (512, BF16), (512, BF16), (512, BF16)]
    return pl.pallas_call(
        functools.partial(_in_proj_kernel, layer=layer),
        grid=(n // tm,),
        in_specs=[row(D_MODEL), _const_spec(wa.shape), _const_spec(wb.shape), _const_spec(wr.shape),
                  _const_spec(wc.shape), _const_spec(gw.shape), _const_spec(gb.shape), _const_spec(lb.shape)],
        out_specs=[row(w) for w, _ in out_widths],
        out_shape=[jax.ShapeDtypeStruct((n, w), dt) for w, dt in out_widths],
        compiler_params=_params(("parallel",)),
        name="in_proj",
    )(x2d, wa, wb, wr, wc, gw, gb, lb)


def _diff_attn_kernel(q_ref, k_ref, v_ref, lam_ref, nw_ref, o_ref, m_scr, l_scr, acc_scr, *, layer):
    t = ATTN_TILE
    qi = pl.program_id(1)

    lane = lax.broadcasted_iota(jnp.int32, (t, LANES), 1)
    qs = []
    for h in range(DIFF_HEADS):
        q = q_ref[0, :, h * LANES:(h + 1) * LANES]
        zero = jnp.zeros_like(q)
        qs.append(jnp.concatenate([jnp.where(lane < DIFF_QK_DIM, q, zero),
                                   jnp.where(lane >= DIFF_QK_DIM, q, zero)], axis=0))

    m_scr[...] = jnp.full(m_scr.shape, -jnp.inf, F32)
    l_scr[...] = jnp.zeros(l_scr.shape, F32)
    acc_scr[...] = jnp.zeros(acc_scr.shape, F32)
    col = lax.broadcasted_iota(jnp.int32, (1, t), 1)
    ones = jnp.ones((t, LANES), BF16)

    def block(j, masked):
        off = pl.multiple_of(j * t, t)
        rel = ((j - qi) * t + col).astype(F32)
        if masked:
            r = lax.broadcasted_iota(jnp.int32, (t, t), 0)
            c = lax.broadcasted_iota(jnp.int32, (t, t), 1)
            keep = jnp.concatenate([r >= c, r >= c], axis=0)
        for h in range(DIFF_HEADS):
            slope = 2.0 ** (-8.0 * (h + 1) / DIFF_HEADS)
            kb = k_ref[0, pl.ds(off, t), h * LANES:(h + 1) * LANES]
            vb = v_ref[0, pl.ds(off, t), h * LANES:(h + 1) * LANES]
            s = _dot_nt(qs[h], kb) + slope * rel
            if masked:
                s = jnp.where(keep, s, MASK_VALUE)
            m_prev = m_scr[h]
            m_next = jnp.maximum(m_prev, jnp.max(s, axis=-1, keepdims=True))
            corr = jnp.exp(m_prev - m_next)
            p = jnp.concatenate([jnp.exp(s[:, b0:b0 + LANES] - m_next) for b0 in range(0, t, LANES)],
                                axis=1).astype(BF16)
            pv = _dot(p, jnp.concatenate([vb, ones], axis=1))
            acc_scr[h] = acc_scr[h] * corr + pv[:, :LANES]
            l_scr[h] = l_scr[h] * corr + pv[:, LANES:]
            m_scr[h] = m_next

    def body(j, carry):
        block(j, False)
        return carry

    lax.fori_loop(0, qi, body, 0)
    block(qi, True)

    lp = lam_ref[...]
    lam_init = 0.8 - 0.6 * math.exp(-0.3 * layer)
    lam = (jnp.exp(jnp.sum(lp[0:1] * lp[1:2], axis=-1, keepdims=True))
           - jnp.exp(jnp.sum(lp[2:3] * lp[3:4], axis=-1, keepdims=True)) + lam_init)
    nw = nw_ref[...]
    for h in range(DIFF_HEADS):
        o12 = acc_scr[h] / l_scr[h]
        o = o12[:t] - lam * o12[t:]
        o_ref[0, :, h * LANES:(h + 1) * LANES] = (_rms_norm(o, nw) * (1.0 - lam_init)).astype(o_ref.dtype)


def _diff_attn(qa, ka, va, lam_p, norm_w, layer):
    b, t, width = qa.shape
    tq = ATTN_TILE
    scr = pltpu.VMEM((DIFF_HEADS, 2 * tq, LANES), F32)
    return pl.pallas_call(
        functools.partial(_diff_attn_kernel, layer=layer),
        grid=(b, t // tq),
        in_specs=[pl.BlockSpec((1, tq, width), lambda bi, qi: (bi, qi, 0)),
                  pl.BlockSpec((1, t, width), lambda bi, qi: (bi, 0, 0)),
                  pl.BlockSpec((1, t, width), lambda bi, qi: (bi, 0, 0)),
                  _const_spec(lam_p.shape), _const_spec(norm_w.shape)],
        out_specs=pl.BlockSpec((1, tq, width), lambda bi, qi: (bi, qi, 0)),
        out_shape=jax.ShapeDtypeStruct((b, t, width), BF16),
        scratch_shapes=[scr, scr, scr],
        compiler_params=_params(("parallel", "arbitrary")),
        name="diff_attn",
    )(qa, ka, va, lam_p, norm_w)


def _level_table(n):
    i = np.arange(n)[:, None]
    j = np.arange(n)[None, :]
    x = np.bitwise_xor(i, j)
    lvl = np.where(i > j, np.floor(np.log2(np.maximum(x, 1))), -1.0)
    return lvl.astype(np.int32)


def _boundary_rows(cum_ref, s):
    c = CHUNK
    if s >= SUBLANES:
        parts = [jnp.broadcast_to(cum_ref[pl.ds(g0 + s - 1, 1), :], (2 * s, LANES))
                 for g0 in range(0, c, 2 * s)]
    elif s == 4:
        parts = [jnp.broadcast_to(cum_ref[pl.ds(g0 + 3, 1), :], (SUBLANES, LANES))
                 for g0 in range(0, c, SUBLANES)]
    else:
        sub = lax.broadcasted_iota(jnp.int32, (SUBLANES, LANES), 0)
        parts = [jnp.where(sub < 4,
                           jnp.broadcast_to(cum_ref[pl.ds(g0 + 1, 1), :], (SUBLANES, LANES)),
                           jnp.broadcast_to(cum_ref[pl.ds(g0 + 5, 1), :], (SUBLANES, LANES)))
                 for g0 in range(0, c, SUBLANES)]
    return jnp.concatenate(parts, axis=0) if len(parts) > 1 else parts[0]


def _gla_kernel(q_ref, k_ref, v_ref, g_ref, og_ref, nw_ref, tri_ref, lvl_ref, o_ref,
                st_scr, cum_scr, *, dk, n_heads):
    heads_per_block = LANES // dk
    n_blocks = n_heads // heads_per_block
    c = CHUNK
    n_levels = int(math.log2(c))

    @pl.when(pl.program_id(1) == 0)
    def _():
        st_scr[...] = jnp.zeros(st_scr.shape, F32)

    tri = tri_ref[...]
    lvl = lvl_ref[...]
    nw = nw_ref[...]
    lane = lax.broadcasted_iota(jnp.int32, (c, LANES), 1)

    for ci in range(REC_ROWS // c):
        r0 = ci * c
        for blk in range(n_blocks):
            l0 = blk * LANES
            g = g_ref[0, r0:r0 + c, l0:l0 + LANES]
            qf = q_ref[0, r0:r0 + c, l0:l0 + LANES].astype(F32)
            kf = k_ref[0, r0:r0 + c, l0:l0 + LANES].astype(F32)
            g_hi, g_mid, g_lo = _split_bf16(g)
            cum = _dot(tri, g_hi) + (_dot(tri, g_mid) + _dot(tri, g_lo))
            cum_scr[...] = cum
            cum_last = cum_scr[pl.ds(c - 1, 1), :]

            q_lv = [(qf * jnp.exp(g)).astype(BF16)]
            k_lv = [kf.astype(BF16)]
            for li in range(1, n_levels):
                d = cum - _boundary_rows(cum_scr, 1 << li)
                q_lv.append((qf * jnp.exp(jnp.minimum(d, 0.0))).astype(BF16))
                k_lv.append((kf * jnp.exp(jnp.minimum(-d, 0.0))).astype(BF16))
            q_in = (qf * jnp.exp(cum)).astype(BF16)
            k_out = (kf * jnp.exp(cum_last - cum)).astype(BF16)
            qk = qf * kf

            for hb in range(heads_per_block):
                h = blk * heads_per_block + hb
                in_head = (lane >= hb * dk) & (lane < (hb + 1) * dk)
                sel = (lambda a: a) if heads_per_block == 1 else (
                    lambda a: jnp.where(in_head, a, jnp.zeros_like(a)))
                vh = v_ref[0, r0:r0 + c, h * LANES:(h + 1) * LANES]
                scores = jnp.zeros((c, c), F32)
                for li in range(n_levels):
                    scores = jnp.where(lvl == li, _dot_nt(sel(q_lv[li]), k_lv[li]), scores)
                diag = jnp.sum(sel(qk), axis=-1, keepdims=True)
                st = st_scr[h]
                o = (_dot(scores.astype(BF16), vh) + diag * vh.astype(F32)
                     + _dot_nt(sel(q_in), st.astype(BF16)))
                st_scr[h] = st * jnp.exp(cum_last) + _dot_tn(vh, k_out)
                og = og_ref[0, r0:r0 + c, h * LANES:(h + 1) * LANES].astype(F32)
                o_ref[0, r0:r0 + c, h * LANES:(h + 1) * LANES] = (_rms_norm(o, nw) * og).astype(o_ref.dtype)


def _gated_linear_attention(q, k, v, g, og, norm_w, dk, n_heads, name):
    b, t, hk = q.shape
    hv = v.shape[-1]
    rows = REC_ROWS
    tri = jnp.asarray(np.tril(np.ones((CHUNK, CHUNK), np.float32)), BF16)
    lvl = jnp.asarray(_level_table(CHUNK))
    seq = lambda width: pl.BlockSpec((1, rows, width), lambda bi, ci: (bi, ci, 0))
    return pl.pallas_call(
        functools.partial(_gla_kernel, dk=dk, n_heads=n_heads),
        grid=(b, t // rows),
        in_specs=[seq(hk), seq(hk), seq(hv), seq(hk), seq(hv),
                  _const_spec(norm_w.shape), _const_spec(tri.shape), _const_spec(lvl.shape)],
        out_specs=seq(hv),
        out_shape=jax.ShapeDtypeStruct((b, t, hv), BF16),
        scratch_shapes=[pltpu.VMEM((n_heads, LANES, LANES), F32), pltpu.VMEM((CHUNK, LANES), F32)],
        compiler_params=_params(("parallel", "arbitrary")),
        name=name,
    )(q, k, v, g, og, norm_w, tri, lvl)


def _merge_kernel(x_ref, oa_ref, ob_ref, oc_ref, wmg_ref, wbr_ref, wout_ref, lnw_ref, lnb_ref, o_ref):
    x = x_ref[...]
    xb = x.astype(BF16)
    mixed = None
    for n, br_ref in enumerate((oa_ref, ob_ref, oc_ref)):
        gate = _sigmoid(_dot(xb, wmg_ref[:, n * D_MODEL:(n + 1) * D_MODEL]))
        y = gate * _dot(br_ref[...], wbr_ref[n])
        mixed = y if mixed is None else mixed + y
    mix = _dot(mixed.astype(BF16), wout_ref[...])
    o_ref[...] = _layer_norm(ALPHA * x + mix, lnw_ref[...], lnb_ref[...])


def _merge(x2d, oa, ob, oc, wmg, wbr, wout, lnw, lnb):
    n = x2d.shape[0]
    tm = min(ROW_TILE, n)
    row = lambda width: pl.BlockSpec((tm, width), lambda i: (i, 0))
    return pl.pallas_call(
        _merge_kernel,
        grid=(n // tm,),
        in_specs=[row(D_MODEL), row(BRANCH_WIDTH), row(BRANCH_WIDTH), row(BRANCH_WIDTH),
                  _const_spec(wmg.shape), _const_spec(wbr.shape), _const_spec(wout.shape),
                  _const_spec(lnw.shape), _const_spec(lnb.shape)],
        out_specs=row(D_MODEL),
        out_shape=jax.ShapeDtypeStruct((n, D_MODEL), F32),
        compiler_params=_params(("parallel",)),
        name="merge",
    )(x2d, oa, ob, oc, wmg, wbr, wout, lnw, lnb)


def _mlp_kernel(x_ref, wup_ref, wdn_ref, lnw_ref, lnb_ref, o_ref):
    x = x_ref[...]
    xb = x.astype(BF16)
    acc = None
    for f0 in range(0, D_FF, D_MODEL):
        hmid = jnp.maximum(_dot(xb, wup_ref[:, f0:f0 + D_MODEL]), 0.0)
        part = _dot((hmid * hmid).astype(BF16), wdn_ref[f0:f0 + D_MODEL, :])
        acc = part if acc is None else acc + part
    o_ref[...] = _layer_norm(ALPHA * x + acc, lnw_ref[...], lnb_ref[...])


def _mlp(x2d, wup, wdn, lnw, lnb):
    n = x2d.shape[0]
    tm = min(ROW_TILE, n)
    row = pl.BlockSpec((tm, D_MODEL), lambda i: (i, 0))
    return pl.pallas_call(
        _mlp_kernel,
        grid=(n // tm,),
        in_specs=[row, _const_spec(wup.shape), _const_spec(wdn.shape),
                  _const_spec(lnw.shape), _const_spec(lnb.shape)],
        out_specs=row,
        out_shape=jax.ShapeDtypeStruct((n, D_MODEL), F32),
        compiler_params=_params(("parallel",)),
        name="mlp",
    )(x2d, wup, wdn, lnw, lnb)


def _layer(x2d, b, t, layer, w_in, gla_w_gate, gla_b_gate, diff_lambda, diff_norm_w, gla_norm_w,
           hgrn_norm_w, hgrn_lb, w_branch, w_out, ln1_w, ln1_b, w_up, w_down, ln2_w, ln2_b):
    n = b * t
    w = w_in[layer]
    wa = w[:, _A_Q:_B_Q].astype(BF16)
    wb = jnp.concatenate([w[:, _B_Q:_B_R], w[:, _B_G:_C_Q]], axis=1).astype(BF16)
    wr = jnp.pad(w[:, _B_R:_B_G], ((0, 0), (0, LANES - GLA_GATE_RANK))).astype(BF16)
    wc = w[:, _C_Q:_MG].astype(BF16)
    wmg = w[:, _MG:].astype(BF16)
    gw = jnp.pad(gla_w_gate[layer], ((0, LANES - GLA_GATE_RANK), (0, 0)))
    gb = gla_b_gate[layer][None, :]

    (qa, ka, va, qg, kg, vg, lg, sgo, qh, lf, kh, ih, sho) = _in_proj(
        x2d, wa, wb, wr, wc, gw, gb, hgrn_lb, layer)
    seq = lambda a: a.reshape(b, t, a.shape[-1])

    oa = _diff_attn(seq(qa), seq(ka), seq(va), diff_lambda[layer], diff_norm_w[layer][None, :], layer)
    ob = _gated_linear_attention(seq(qg), seq(kg), seq(vg), seq(lg), seq(sgo),
                                 gla_norm_w[layer][None, :], GLA_DK, GLA_HEADS, "gla")
    oc = _gated_linear_attention(seq(qh), seq(kh), seq(ih), seq(lf), seq(sho),
                                 hgrn_norm_w[layer][None, :], HGRN_EXPAND, HGRN_HEADS, "hgrn")

    flat = lambda a: a.reshape(n, a.shape[-1])
    x1 = _merge(x2d, flat(oa), flat(ob), flat(oc), wmg, w_branch[layer].astype(BF16),
                w_out[layer].astype(BF16), ln1_w[layer][None, :], ln1_b[layer][None, :])
    return _mlp(x1, w_up[layer].astype(BF16), w_down[layer].astype(BF16),
                ln2_w[layer][None, :], ln2_b[layer][None, :])


def kernel(x, w_in, gla_w_gate, gla_b_gate, diff_lambda, diff_norm_w, gla_norm_w, hgrn_norm_w, hgrn_lb,
           w_branch, w_out, ln1_w, ln1_b, w_up, w_down, ln2_w, ln2_b):
    b, t, d = x.shape
    assert d == D_MODEL and w_in.shape == (DEPTH, D_MODEL, IN_WIDTH)
    assert t % REC_ROWS == 0 and t % ATTN_TILE == 0 and (b * t) % min(ROW_TILE, b * t) == 0
    x2d = x.reshape(b * t, d)
    for layer in range(DEPTH):
        x2d = _layer(x2d, b, t, layer, w_in, gla_w_gate, gla_b_gate, diff_lambda, diff_norm_w,
                     gla_norm_w, hgrn_norm_w, hgrn_lb, w_branch, w_out, ln1_w, ln1_b,
                     w_up, w_down, ln2_w, ln2_b)
    return x2d.reshape(b, t, d)
```

```python
import functools
import math

import numpy as np
import jax
import jax.numpy as jnp
from jax import lax
from jax.experimental import pallas as pl
from jax.experimental.pallas import tpu as pltpu

D_MODEL = 1024
DEPTH = 2
DIFF_HEADS = 4
DIFF_QK_DIM = 64
DIFF_V_DIM = 128
GLA_HEADS = 4
GLA_DK = 64
GLA_DV = 128
GLA_GATE_RANK = 16
GLA_GATE_NORM = 16.0
HGRN_HEADS = 4
HGRN_EXPAND = 128
HGRN_DV = 128
N_BRANCHES = 3
BRANCH_WIDTH = 512
D_FF = 4 * D_MODEL
ALPHA = (2 * DEPTH) ** 0.25
LN_EPS = 1e-5
MASK_VALUE = -1e30
LB_FLOOR = 1e-30

LANES = 128
SUBLANES = 8
VMEM_LIMIT_BYTES = 56 * 1024 * 1024

ROW_TILE = 512
SUB_ROWS = 256
ATTN_TILE = 512
CHUNK = 128
REC_ROWS = 1024

F32 = jnp.float32
BF16 = jnp.bfloat16

_A_Q, _A_K, _A_V = 0, 512, 1024
_B_Q, _B_K, _B_V, _B_R, _B_G = 1536, 1792, 2048, 2560, 2576
_C_Q, _C_F, _C_I, _C_G = 3088, 3600, 4112, 4624
_MG = 5136
IN_WIDTH = _MG + N_BRANCHES * D_MODEL


def _dot(a, b):
    return jnp.dot(a, b, preferred_element_type=F32)


def _dot_nt(a, b):
    return lax.dot_general(a, b, (((1,), (1,)), ((), ())), preferred_element_type=F32)


def _dot_tn(a, b):
    return lax.dot_general(a, b, (((0,), (0,)), ((), ())), preferred_element_type=F32)


def _split_bf16(a):
    hi = a.astype(BF16)
    r = a - hi.astype(F32)
    mid = r.astype(BF16)
    lo = (r - mid.astype(F32)).astype(BF16)
    return hi, mid, lo


def _log_sigmoid(u):
    return jnp.minimum(u, 0.0) - jnp.log1p(jnp.exp(-jnp.abs(u)))


def _sigmoid(u):
    return 1.0 / (1.0 + jnp.exp(-u))


def _layer_norm(r, w, b):
    mu = jnp.mean(r, axis=-1, keepdims=True)
    c = r - mu
    var = jnp.mean(c * c, axis=-1, keepdims=True)
    return c * lax.rsqrt(var + LN_EPS) * w + b


def _rms_norm(o, w):
    return o * lax.rsqrt(jnp.mean(o * o, axis=-1, keepdims=True) + LN_EPS) * w


def _const_spec(shape):
    nd = len(shape)
    return pl.BlockSpec(shape, lambda *_: (0,) * nd, pipeline_mode=pl.Buffered(1))


def _params(semantics):
    return pltpu.CompilerParams(dimension_semantics=semantics, vmem_limit_bytes=VMEM_LIMIT_BYTES)


def _in_proj_kernel(x_ref, wa_ref, wb_ref, wr_ref, wc_ref, gw_ref, gb_ref, lb_ref,
                    qa_ref, ka_ref, va_ref, qg_ref, kg_ref, vg_ref, lg_ref, sgo_ref,
                    qh_ref, lf_ref, kh_ref, ih_ref, sho_ref, *, layer):
    lb_rows = [lb_ref[d:d + 1, :] for d in range(DEPTH)]
    mx = functools.reduce(jnp.maximum, lb_rows)
    es = [jnp.exp(r - mx) for r in lb_rows]
    tot = functools.reduce(jnp.add, es)
    soft = [e / tot for e in es]
    lb = functools.reduce(jnp.add, soft[:layer + 1]) - soft[0]
    log_lb = jnp.log(jnp.maximum(lb, LB_FLOOR))
    log_1m_lb = jnp.log1p(-lb)
    w_hi, w_mid, _ = _split_bf16(gw_ref[...])

    for r0 in range(0, x_ref.shape[0], SUB_ROWS):
        rows = slice(r0, r0 + SUB_ROWS)
        xb = x_ref[rows, :].astype(BF16)

        def proj(w_ref, lo, width):
            return _dot(xb, w_ref[:, lo:lo + width])

        b2 = log_1m_lb + _log_sigmoid(proj(wc_ref, 512, 512))
        log_f = jnp.maximum(log_lb, b2) + jnp.log1p(jnp.exp(-jnp.abs(log_lb - b2)))
        lf_ref[rows, :] = log_f
        kh_ref[rows, :] = (1.0 - jnp.exp(log_f)).astype(BF16)
        qa_ref[rows, :] = (proj(wa_ref, 0, 512) * DIFF_QK_DIM ** -0.5).astype(BF16)
        ka_ref[rows, :] = proj(wa_ref, 512, 512).astype(BF16)
        gout = proj(wb_ref, 1024, 512)
        sgo_ref[rows, :] = (gout * _sigmoid(gout)).astype(BF16)
        va_ref[rows, :] = proj(wa_ref, 1024, 512).astype(BF16)
        qh_ref[rows, :] = proj(wc_ref, 0, 512).astype(BF16)
        hout = proj(wc_ref, 1536, 512)
        sho_ref[rows, :] = (hout * _sigmoid(hout)).astype(BF16)
        ih_ref[rows, :] = proj(wc_ref, 1024, 512).astype(BF16)
        vg_ref[rows, :] = proj(wb_ref, 512, 512).astype(BF16)
        glr = proj(wr_ref, 0, LANES)
        a_hi, a_mid, _ = _split_bf16(glr)
        u = _dot(a_hi, w_hi) + (_dot(a_hi, w_mid) + _dot(a_mid, w_hi)) + gb_ref[...]
        lg_ref[rows, :] = _log_sigmoid(u) * (1.0 / GLA_GATE_NORM)
        qg_ref[rows, :] = (proj(wb_ref, 0, 256) * GLA_DK ** -0.5).astype(BF16)
        kg_ref[rows, :] = proj(wb_ref, 256, 256).astype(BF16)


def _in_proj(x2d, wa, wb, wr, wc, gw, gb, lb, layer):
    n = x2d.shape[0]
    tm = min(ROW_TILE, n)
    row = lambda width: pl.BlockSpec((tm, width), lambda i: (i, 0))
    out_widths = [(512, BF16), (512, BF16), (512, BF16),
                  (256, BF16), (256, BF16), (512, BF16), (256, F32), (512, BF16),
                  (512, BF16), (512, F32), (512, BF16), (512, BF16), (512, BF16)]
    return pl.pallas_call(
        functools.partial(_in_proj_kernel, layer=layer),
        grid=(n // tm,),
        in_specs=[row(D_MODEL), _const_spec(wa.shape), _const_spec(wb.shape), _const_spec(wr.shape),
                  _const_spec(wc.shape), _const_spec(gw.shape), _const_spec(gb.shape), _const_spec(lb.shape)],
        out_specs=[row(w) for w, _ in out_widths],
        out_shape=[jax.ShapeDtypeStruct((n, w), dt) for w, dt in out_widths],
        compiler_params=_params(("parallel",)),
        name="in_proj",
    )(x2d, wa, wb, wr, wc, gw, gb, lb)


def _diff_attn_kernel(q_ref, k_ref, v_ref, lam_ref, nw_ref, o_ref, m_scr, l_scr, acc_scr, *, layer):
    t = ATTN_TILE
    qi = pl.program_id(1)

    lane = lax.broadcasted_iota(jnp.int32, (t, LANES), 1)
    qs = []
    for h in range(DIFF_HEADS):
        q = q_ref[0, :, h * LANES:(h + 1) * LANES]
        zero = jnp.zeros_like(q)
        qs.append(jnp.concatenate([jnp.where(lane < DIFF_QK_DIM, q, zero),
                                   jnp.where(lane >= DIFF_QK_DIM, q, zero)], axis=0))

    m_scr[...] = jnp.full(m_scr.shape, -jnp.inf, F32)
    l_scr[...] = jnp.zeros(l_scr.shape, F32)
    acc_scr[...] = jnp.zeros(acc_scr.shape, F32)
    col = lax.broadcasted_iota(jnp.int32, (1, t), 1)
    ones = jnp.ones((t, LANES), BF16)

    def block(j, masked):
        off = pl.multiple_of(j * t, t)
        rel = ((j - qi) * t + col).astype(F32)
        if masked:
            r = lax.broadcasted_iota(jnp.int32, (t, t), 0)
            c = lax.broadcasted_iota(jnp.int32, (t, t), 1)
            keep = jnp.concatenate([r >= c, r >= c], axis=0)
        for h in range(DIFF_HEADS):
            slope = 2.0 ** (-8.0 * (h + 1) / DIFF_HEADS)
            kb = k_ref[0, pl.ds(off, t), h * LANES:(h + 1) * LANES]
            vb = v_ref[0, pl.ds(off, t), h * LANES:(h + 1) * LANES]
            s = _dot_nt(qs[h], kb) + slope * rel
            if masked:
                s = jnp.where(keep, s, MASK_VALUE)
            m_prev = m_scr[h]
            m_next = jnp.maximum(m_prev, jnp.max(s, axis=-1, keepdims=True))
            corr = jnp.exp(m_prev - m_next)
            p = jnp.concatenate([jnp.exp(s[:, b0:b0 + LANES] - m_next) for b0 in range(0, t, LANES)],
                                axis=1).astype(BF16)
            pv = _dot(p, jnp.concatenate([vb, ones], axis=1))
            acc_scr[h] = acc_scr[h] * corr + pv[:, :LANES]
            l_scr[h] = l_scr[h] * corr + pv[:, LANES:]
            m_scr[h] = m_next

    def body(j, carry):
        block(j, False)
        return carry

    lax.fori_loop(0, qi, body, 0)
    block(qi, True)

    lp = lam_ref[...]
    lam_init = 0.8 - 0.6 * math.exp(-0.3 * layer)
    lam = (jnp.exp(jnp.sum(lp[0:1] * lp[1:2], axis=-1, keepdims=True))
           - jnp.exp(jnp.sum(lp[2:3] * lp[3:4], axis=-1, keepdims=True)) + lam_init)
    nw = nw_ref[...]
    for h in range(DIFF_HEADS):
        o12 = acc_scr[h] / l_scr[h]
        o = o12[:t] - lam * o12[t:]
        o_ref[0, :, h * LANES:(h + 1) * LANES] = (_rms_norm(o, nw) * (1.0 - lam_init)).astype(o_ref.dtype)


def _diff_attn(qa, ka, va, lam_p, norm_w, layer):
    b, t, width = qa.shape
    tq = ATTN_TILE
    scr = pltpu.VMEM((DIFF_HEADS, 2 * tq, LANES), F32)
    return pl.pallas_call(
        functools.partial(_diff_attn_kernel, layer=layer),
        grid=(b, t // tq),
        in_specs=[pl.BlockSpec((1, tq, width), lambda bi, qi: (bi, qi, 0)),
                  pl.BlockSpec((1, t, width), lambda bi, qi: (bi, 0, 0)),
                  pl.BlockSpec((1, t, width), lambda bi, qi: (bi, 0, 0)),
                  _const_spec(lam_p.shape), _const_spec(norm_w.shape)],
        out_specs=pl.BlockSpec((1, tq, width), lambda bi, qi: (bi, qi, 0)),
        out_shape=jax.ShapeDtypeStruct((b, t, width), BF16),
        scratch_shapes=[scr, scr, scr],
        compiler_params=_params(("parallel", "arbitrary")),
        name="diff_attn",
    )(qa, ka, va, lam_p, norm_w)


_TILE_LEVEL0 = 3
_N_LEVELS = int(math.log2(CHUNK))
_N_TILES = CHUNK // SUBLANES
_ROW_P = (_N_LEVELS - _TILE_LEVEL0) * _N_TILES
_ROW_Q = _ROW_P + _N_TILES
_ROW_LAST = _ROW_Q + _N_TILES
_PACKED_ROWS = -(-(_ROW_LAST + 1) // 16) * 16


def _level_table(n):
    i = np.arange(n)[:, None]
    j = np.arange(n)[None, :]
    x = np.bitwise_xor(i, j)
    lvl = np.where(i > j, np.floor(np.log2(np.maximum(x, 1))), -1.0)
    return lvl.astype(np.int32)


def _tile_factor_row(li):
    return (li - _TILE_LEVEL0) * _N_TILES


def _gate_sum_matrix():
    c, nt, sl = CHUNK, _N_TILES, SUBLANES
    m = np.zeros((c + _PACKED_ROWS, c), np.float32)
    m[:c] = np.tril(np.ones((c, c), np.float32))
    for li in range(_TILE_LEVEL0, _N_LEVELS):
        st = (1 << li) // sl
        for tile in range(nt):
            first = (tile // (2 * st)) * 2 * st
            boundary = sl * (first + st)
            if tile - first >= st:
                m[c + _tile_factor_row(li) + tile, boundary:sl * tile] = 1.0
            else:
                m[c + _tile_factor_row(li) + tile, sl * tile + sl:boundary] = 1.0
    for tile in range(nt):
        m[c + _ROW_P + tile, :sl * tile] = 1.0
        m[c + _ROW_Q + tile, sl * tile + sl:] = 1.0
    m[c + _ROW_LAST, :] = 1.0
    return np.concatenate([m, m], axis=1)


def _tile_rows(ref, blk, first_row, stride):
    return jnp.concatenate(
        [jnp.broadcast_to(ref[blk, pl.ds(first_row + stride * tile, 1), :], (SUBLANES, LANES))
         for tile in range(_N_TILES)], axis=0)


def _gla_kernel(q_ref, k_ref, v_ref, g_ref, og_ref, nw_ref, gsum_ref, lvl_ref, o_ref,
                st_scr, cum_scr, fac_scr, *, dk, n_heads):
    heads_per_block = LANES // dk
    n_blocks = n_heads // heads_per_block
    c = CHUNK
    sl = SUBLANES
    pad = SUBLANES

    @pl.when(pl.program_id(1) == 0)
    def _():
        st_scr[...] = jnp.zeros(st_scr.shape, F32)

    cum_scr[:, 0:pad, :] = jnp.zeros((n_blocks, pad, LANES), F32)
    gsum = gsum_ref[...]
    lvl = lvl_ref[...]
    nw = nw_ref[...]
    lane = lax.broadcasted_iota(jnp.int32, (c, LANES), 1)
    sub = lax.broadcasted_iota(jnp.int32, (sl, LANES), 0)

    def chunk(ci, carry):
        rows = pl.ds(pl.multiple_of(ci * c, c), c)
        for blk in range(n_blocks):
            lanes = slice(blk * LANES, (blk + 1) * LANES)
            g = g_ref[0, rows, lanes]
            kb = k_ref[0, rows, lanes]
            qf = q_ref[0, rows, lanes].astype(F32)
            kf = kb.astype(F32)
            g_hi, g_mid, _ = _split_bf16(g)
            sums = _dot(gsum, jnp.concatenate([g_hi, g_mid], axis=0))
            cum = sums[:c]
            cum_scr[blk, pad:pad + c, :] = cum
            fac_scr[blk] = jnp.exp(sums[c:])

            q_lv = [(qf * jnp.exp(g)).astype(BF16)]
            k_lv = [kb]
            m2 = jnp.concatenate(
                [jnp.where(sub < 4,
                           jnp.broadcast_to(cum_scr[blk, pl.ds(pad + sl * t + 1, 1), :], (sl, LANES)),
                           jnp.broadcast_to(cum_scr[blk, pl.ds(pad + sl * t + 5, 1), :], (sl, LANES)))
                 for t in range(_N_TILES)], axis=0)
            for m in (m2, _tile_rows(cum_scr, blk, pad + 3, sl)):
                e = jnp.exp(-jnp.abs(cum - m))
                q_lv.append((qf * e).astype(BF16))
                k_lv.append((kf * e).astype(BF16))
            q_t = qf * jnp.exp(cum - _tile_rows(cum_scr, blk, pad - 1, sl))
            k_t = kf * jnp.exp(_tile_rows(cum_scr, blk, pad + sl - 1, sl) - cum)
            for li in range(_TILE_LEVEL0, _N_LEVELS):
                f = _tile_rows(fac_scr, blk, _tile_factor_row(li), 1)
                q_lv.append((q_t * f).astype(BF16))
                k_lv.append((k_t * f).astype(BF16))
            q_in = (q_t * _tile_rows(fac_scr, blk, _ROW_P, 1)).astype(BF16)
            k_out = (k_t * _tile_rows(fac_scr, blk, _ROW_Q, 1)).astype(BF16)
            decay = fac_scr[blk, pl.ds(_ROW_LAST, 1), :]
            qk = qf * kf

            for hb in range(heads_per_block):
                h = blk * heads_per_block + hb
                in_head = (lane >= hb * dk) & (lane < (hb + 1) * dk)
                sel = (lambda a: a) if heads_per_block == 1 else (
                    lambda a: jnp.where(in_head, a, jnp.zeros_like(a)))
                hl = slice(h * LANES, (h + 1) * LANES)
                vh = v_ref[0, rows, hl]
                scores = jnp.zeros((c, c), F32)
                for li in range(_N_LEVELS):
                    scores = jnp.where(lvl == li, _dot_nt(sel(q_lv[li]), k_lv[li]), scores)
                diag = jnp.sum(sel(qk), axis=-1, keepdims=True)
                st = st_scr[h]
                o = (_dot(scores.astype(BF16), vh) + diag * vh.astype(F32)
                     + _dot_nt(sel(q_in), st.astype(BF16)))
                st_scr[h] = st * decay + _dot_tn(vh, k_out)
                og = og_ref[0, rows, hl].astype(F32)
                o_ref[0, rows, hl] = (_rms_norm(o, nw) * og).astype(o_ref.dtype)
        return carry

    lax.fori_loop(0, REC_ROWS // c, chunk, 0, unroll=2)


def _gated_linear_attention(q, k, v, g, og, norm_w, dk, n_heads, name):
    b, t, hk = q.shape
    hv = v.shape[-1]
    rows = REC_ROWS
    n_blocks = hk // LANES
    gsum = jnp.asarray(_gate_sum_matrix(), BF16)
    lvl = jnp.asarray(_level_table(CHUNK))
    seq = lambda width: pl.BlockSpec((1, rows, width), lambda bi, ci: (bi, ci, 0))
    return pl.pallas_call(
        functools.partial(_gla_kernel, dk=dk, n_heads=n_heads),
        grid=(b, t // rows),
        in_specs=[seq(hk), seq(hk), seq(hv), seq(hk), seq(hv),
                  _const_spec(norm_w.shape), _const_spec(gsum.shape), _const_spec(lvl.shape)],
        out_specs=seq(hv),
        out_shape=jax.ShapeDtypeStruct((b, t, hv), BF16),
        scratch_shapes=[pltpu.VMEM((n_heads, LANES, LANES), F32),
                        pltpu.VMEM((n_blocks, SUBLANES + CHUNK, LANES), F32),
                        pltpu.VMEM((n_blocks, _PACKED_ROWS, LANES), F32)],
        compiler_params=_params(("parallel", "arbitrary")),
        name=name,
    )(q, k, v, g, og, norm_w, gsum, lvl)


def _merge_kernel(x_ref, oa_ref, ob_ref, oc_ref, wmg_ref, wbr_ref, wout_ref, lnw_ref, lnb_ref, o_ref):
    for r0 in range(0, x_ref.shape[0], SUB_ROWS):
        rows = slice(r0, r0 + SUB_ROWS)
        x = x_ref[rows, :]
        xb = x.astype(BF16)
        mixed = None
        for n, br_ref in enumerate((oa_ref, ob_ref, oc_ref)):
            gate = _sigmoid(_dot(xb, wmg_ref[:, n * D_MODEL:(n + 1) * D_MODEL]))
            y = gate * _dot(br_ref[rows, :], wbr_ref[n])
            mixed = y if mixed is None else mixed + y
        mix = _dot(mixed.astype(BF16), wout_ref[...])
        o_ref[rows, :] = _layer_norm(ALPHA * x + mix, lnw_ref[...], lnb_ref[...])


def _merge(x2d, oa, ob, oc, wmg, wbr, wout, lnw, lnb):
    n = x2d.shape[0]
    tm = min(ROW_TILE, n)
    row = lambda width: pl.BlockSpec((tm, width), lambda i: (i, 0))
    return pl.pallas_call(
        _merge_kernel,
        grid=(n // tm,),
        in_specs=[row(D_MODEL), row(BRANCH_WIDTH), row(BRANCH_WIDTH), row(BRANCH_WIDTH),
                  _const_spec(wmg.shape), _const_spec(wbr.shape), _const_spec(wout.shape),
                  _const_spec(lnw.shape), _const_spec(lnb.shape)],
        out_specs=row(D_MODEL),
        out_shape=jax.ShapeDtypeStruct((n, D_MODEL), F32),
        compiler_params=_params(("parallel",)),
        name="merge",
    )(x2d, oa, ob, oc, wmg, wbr, wout, lnw, lnb)


def _mlp_kernel(x_ref, wup_ref, wdn_ref, lnw_ref, lnb_ref, o_ref):
    for r0 in range(0, x_ref.shape[0], SUB_ROWS):
        rows = slice(r0, r0 + SUB_ROWS)
        x = x_ref[rows, :]
        xb = x.astype(BF16)
        acc = None
        for f0 in range(0, D_FF, D_MODEL):
            hmid = jnp.maximum(_dot(xb, wup_ref[:, f0:f0 + D_MODEL]), 0.0)
            part = _dot((hmid * hmid).astype(BF16), wdn_ref[f0:f0 + D_MODEL, :])
            acc = part if acc is None else acc + part
        o_ref[rows, :] = _layer_norm(ALPHA * x + acc, lnw_ref[...], lnb_ref[...])


def _mlp(x2d, wup, wdn, lnw, lnb):
    n = x2d.shape[0]
    tm = min(ROW_TILE, n)
    row = pl.BlockSpec((tm, D_MODEL), lambda i: (i, 0))
    return pl.pallas_call(
        _mlp_kernel,
        grid=(n // tm,),
        in_specs=[row, _const_spec(wup.shape), _const_spec(wdn.shape),
                  _const_spec(lnw.shape), _const_spec(lnb.shape)],
        out_specs=row,
        out_shape=jax.ShapeDtypeStruct((n, D_MODEL), F32),
        compiler_params=_params(("parallel",)),
        name="mlp",
    )(x2d, wup, wdn, lnw, lnb)


def _layer(x2d, b, t, layer, w_in, gla_w_gate, gla_b_gate, diff_lambda, diff_norm_w, gla_norm_w,
           hgrn_norm_w, hgrn_lb, w_branch, w_out, ln1_w, ln1_b, w_up, w_down, ln2_w, ln2_b):
    n = b * t
    w = w_in[layer]
    wa = w[:, _A_Q:_B_Q].astype(BF16)
    wb = jnp.concatenate([w[:, _B_Q:_B_R], w[:, _B_G:_C_Q]], axis=1).astype(BF16)
    wr = jnp.pad(w[:, _B_R:_B_G], ((0, 0), (0, LANES - GLA_GATE_RANK))).astype(BF16)
    wc = w[:, _C_Q:_MG].astype(BF16)
    wmg = w[:, _MG:].astype(BF16)
    gw = jnp.pad(gla_w_gate[layer], ((0, LANES - GLA_GATE_RANK), (0, 0)))
    gb = gla_b_gate[layer][None, :]

    (qa, ka, va, qg, kg, vg, lg, sgo, qh, lf, kh, ih, sho) = _in_proj(
        x2d, wa, wb, wr, wc, gw, gb, hgrn_lb, layer)
    seq = lambda a: a.reshape(b, t, a.shape[-1])

    oa = _diff_attn(seq(qa), seq(ka), seq(va), diff_lambda[layer], diff_norm_w[layer][None, :], layer)
    ob = _gated_linear_attention(seq(qg), seq(kg), seq(vg), seq(lg), seq(sgo),
                                 gla_norm_w[layer][None, :], GLA_DK, GLA_HEADS, "gla")
    oc = _gated_linear_attention(seq(qh), seq(kh), seq(ih), seq(lf), seq(sho),
                                 hgrn_norm_w[layer][None, :], HGRN_EXPAND, HGRN_HEADS, "hgrn")

    flat = lambda a: a.reshape(n, a.shape[-1])
    x1 = _merge(x2d, flat(oa), flat(ob), flat(oc), wmg, w_branch[layer].astype(BF16),
                w_out[layer].astype(BF16), ln1_w[layer][None, :], ln1_b[layer][None, :])
    return _mlp(x1, w_up[layer].astype(BF16), w_down[layer].astype(BF16),
                ln2_w[layer][None, :], ln2_b[layer][None, :])


def kernel(x, w_in, gla_w_gate, gla_b_gate, diff_lambda, diff_norm_w, gla_norm_w, hgrn_norm_w, hgrn_lb,
           w_branch, w_out, ln1_w, ln1_b, w_up, w_down, ln2_w, ln2_b):
    b, t, d = x.shape
    assert d == D_MODEL and w_in.shape == (DEPTH, D_MODEL, IN_WIDTH)
    assert t % REC_ROWS == 0 and t % ATTN_TILE == 0 and (b * t) % ROW_TILE == 0 and ROW_TILE % SUB_ROWS == 0
    x2d = x.reshape(b * t, d)
    for layer in range(DEPTH):
        x2d = _layer(x2d, b, t, layer, w_in, gla_w_gate, gla_b_gate, diff_lambda, diff_norm_w,
                     gla_norm_w, hgrn_norm_w, hgrn_lb, w_branch, w_out, ln1_w, ln1_b,
                     w_up, w_down, ln2_w, ln2_b)
    return x2d.reshape(b, t, d)
```

```python
import functools
import math

import numpy as np
import jax
import jax.numpy as jnp
from jax import lax
from jax.experimental import pallas as pl
from jax.experimental.pallas import tpu as pltpu

D_MODEL = 1024
DEPTH = 2
DIFF_HEADS = 4
DIFF_QK_DIM = 64
DIFF_V_DIM = 128
GLA_HEADS = 4
GLA_DK = 64
GLA_DV = 128
GLA_GATE_RANK = 16
GLA_GATE_NORM = 16.0
HGRN_HEADS = 4
HGRN_EXPAND = 128
HGRN_DV = 128
N_BRANCHES = 3
BRANCH_WIDTH = 512
D_FF = 4 * D_MODEL
ALPHA = (2 * DEPTH) ** 0.25
LN_EPS = 1e-5
MASK_VALUE = -1e30
LB_FLOOR = 1e-30

LANES = 128
SUBLANES = 8
VMEM_LIMIT_BYTES = 56 * 1024 * 1024

ROW_TILE = 512
SUB_ROWS = 256
ATTN_TILE = 512
CHUNK = 128
REC_ROWS = 1024
FAST_PATH_MAX_DECAY = 60.0
GLA_FAST_UNIT = 128
HGRN_FAST_UNIT = 32

F32 = jnp.float32
BF16 = jnp.bfloat16

_A_Q, _A_K, _A_V = 0, 512, 1024
_B_Q, _B_K, _B_V, _B_R, _B_G = 1536, 1792, 2048, 2560, 2576
_C_Q, _C_F, _C_I, _C_G = 3088, 3600, 4112, 4624
_MG = 5136
IN_WIDTH = _MG + N_BRANCHES * D_MODEL


def _dot(a, b):
    return jnp.dot(a, b, preferred_element_type=F32)


def _dot_nt(a, b):
    return lax.dot_general(a, b, (((1,), (1,)), ((), ())), preferred_element_type=F32)


def _dot_tn(a, b):
    return lax.dot_general(a, b, (((0,), (0,)), ((), ())), preferred_element_type=F32)


def _split_bf16(a):
    hi = a.astype(BF16)
    r = a - hi.astype(F32)
    mid = r.astype(BF16)
    lo = (r - mid.astype(F32)).astype(BF16)
    return hi, mid, lo


def _log_sigmoid(u):
    return jnp.minimum(u, 0.0) - jnp.log1p(jnp.exp(-jnp.abs(u)))


def _sigmoid(u):
    return 1.0 / (1.0 + jnp.exp(-u))


def _layer_norm(r, w, b):
    mu = jnp.mean(r, axis=-1, keepdims=True)
    c = r - mu
    var = jnp.mean(c * c, axis=-1, keepdims=True)
    return c * lax.rsqrt(var + LN_EPS) * w + b


def _rms_norm(o, w):
    return o * lax.rsqrt(jnp.mean(o * o, axis=-1, keepdims=True) + LN_EPS) * w


def _const_spec(shape):
    nd = len(shape)
    return pl.BlockSpec(shape, lambda *_: (0,) * nd, pipeline_mode=pl.Buffered(1))


def _params(semantics):
    return pltpu.CompilerParams(dimension_semantics=semantics, vmem_limit_bytes=VMEM_LIMIT_BYTES)


def _in_proj_kernel(x_ref, wa_ref, wb_ref, wr_ref, wc_ref, gw_ref, gb_ref, lb_ref,
                    qa_ref, ka_ref, va_ref, qg_ref, kg_ref, vg_ref, lg_ref, sgo_ref,
                    qh_ref, lf_ref, kh_ref, ih_ref, sho_ref, *, layer):
    lb_rows = [lb_ref[d:d + 1, :] for d in range(DEPTH)]
    mx = functools.reduce(jnp.maximum, lb_rows)
    es = [jnp.exp(r - mx) for r in lb_rows]
    tot = functools.reduce(jnp.add, es)
    soft = [e / tot for e in es]
    lb = functools.reduce(jnp.add, soft[:layer + 1]) - soft[0]
    log_lb = jnp.log(jnp.maximum(lb, LB_FLOOR))
    log_1m_lb = jnp.log1p(-lb)
    w_hi, w_mid, _ = _split_bf16(gw_ref[...])

    for r0 in range(0, x_ref.shape[0], SUB_ROWS):
        rows = slice(r0, r0 + SUB_ROWS)
        xb = x_ref[rows, :].astype(BF16)

        def proj(w_ref, lo, width):
            return _dot(xb, w_ref[:, lo:lo + width])

        b2 = log_1m_lb + _log_sigmoid(proj(wc_ref, 512, 512))
        log_f = jnp.maximum(log_lb, b2) + jnp.log1p(jnp.exp(-jnp.abs(log_lb - b2)))
        lf_ref[rows, :] = log_f
        kh_ref[rows, :] = (1.0 - jnp.exp(log_f)).astype(BF16)
        qa_ref[rows, :] = (proj(wa_ref, 0, 512) * DIFF_QK_DIM ** -0.5).astype(BF16)
        ka_ref[rows, :] = proj(wa_ref, 512, 512).astype(BF16)
        gout = proj(wb_ref, 1024, 512)
        sgo_ref[rows, :] = (gout * _sigmoid(gout)).astype(BF16)
        va_ref[rows, :] = proj(wa_ref, 1024, 512).astype(BF16)
        qh_ref[rows, :] = proj(wc_ref, 0, 512).astype(BF16)
        hout = proj(wc_ref, 1536, 512)
        sho_ref[rows, :] = (hout * _sigmoid(hout)).astype(BF16)
        ih_ref[rows, :] = proj(wc_ref, 1024, 512).astype(BF16)
        vg_ref[rows, :] = proj(wb_ref, 512, 512).astype(BF16)
        glr = proj(wr_ref, 0, LANES)
        a_hi, a_mid, _ = _split_bf16(glr)
        u = _dot(a_hi, w_hi) + (_dot(a_hi, w_mid) + _dot(a_mid, w_hi)) + gb_ref[...]
        lg_ref[rows, :] = _log_sigmoid(u) * (1.0 / GLA_GATE_NORM)
        qg_ref[rows, :] = (proj(wb_ref, 0, 256) * GLA_DK ** -0.5).astype(BF16)
        kg_ref[rows, :] = proj(wb_ref, 256, 256).astype(BF16)


def _in_proj(x2d, wa, wb, wr, wc, gw, gb, lb, layer):
    n = x2d.shape[0]
    tm = min(ROW_TILE, n)
    row = lambda width: pl.BlockSpec((tm, width), lambda i: (i, 0))
    out_widths = [(512, BF16), (512, BF16), (512, BF16),
                  (256, BF16), (256, BF16), (512, BF16), (256, F32), (512, BF16),
                  (512, BF16), (512, F32), (512, BF16), (512, BF16), (512, BF16)]
    return pl.pallas_call(
        functools.partial(_in_proj_kernel, layer=layer),
        grid=(n // tm,),
        in_specs=[row(D_MODEL), _const_spec(wa.shape), _const_spec(wb.shape), _const_spec(wr.shape),
                  _const_spec(wc.shape), _const_spec(gw.shape), _const_spec(gb.shape), _const_spec(lb.shape)],
        out_specs=[row(w) for w, _ in out_widths],
        out_shape=[jax.ShapeDtypeStruct((n, w), dt) for w, dt in out_widths],
        compiler_params=_params(("parallel",)),
        name="in_proj",
    )(x2d, wa, wb, wr, wc, gw, gb, lb)


def _diff_attn_kernel(q_ref, k_ref, v_ref, lam_ref, nw_ref, o_ref, m_scr, l_scr, acc_scr, *, layer):
    t = ATTN_TILE
    qi = pl.program_id(1)

    lane = lax.broadcasted_iota(jnp.int32, (t, LANES), 1)
    qs = []
    for h in range(DIFF_HEADS):
        q = q_ref[0, :, h * LANES:(h + 1) * LANES]
        zero = jnp.zeros_like(q)
        qs.append(jnp.concatenate([jnp.where(lane < DIFF_QK_DIM, q, zero),
                                   jnp.where(lane >= DIFF_QK_DIM, q, zero)], axis=0))

    m_scr[...] = jnp.full(m_scr.shape, -jnp.inf, F32)
    l_scr[...] = jnp.zeros(l_scr.shape, F32)
    acc_scr[...] = jnp.zeros(acc_scr.shape, F32)
    col = lax.broadcasted_iota(jnp.int32, (1, t), 1)
    ones = jnp.ones((t, LANES), BF16)

    def block(j, masked):
        off = pl.multiple_of(j * t, t)
        rel = ((j - qi) * t + col).astype(F32)
        if masked:
            r = lax.broadcasted_iota(jnp.int32, (t, t), 0)
            c = lax.broadcasted_iota(jnp.int32, (t, t), 1)
            keep = jnp.concatenate([r >= c, r >= c], axis=0)
        for h in range(DIFF_HEADS):
            slope = 2.0 ** (-8.0 * (h + 1) / DIFF_HEADS)
            kb = k_ref[0, pl.ds(off, t), h * LANES:(h + 1) * LANES]
            vb = v_ref[0, pl.ds(off, t), h * LANES:(h + 1) * LANES]
            s = _dot_nt(qs[h], kb) + slope * rel
            if masked:
                s = jnp.where(keep, s, MASK_VALUE)
            m_prev = m_scr[h]
            m_next = jnp.maximum(m_prev, jnp.max(s, axis=-1, keepdims=True))
            corr = jnp.exp(m_prev - m_next)
            p = jnp.concatenate([jnp.exp(s[:, b0:b0 + LANES] - m_next) for b0 in range(0, t, LANES)],
                                axis=1).astype(BF16)
            pv = _dot(p, jnp.concatenate([vb, ones], axis=1))
            acc_scr[h] = acc_scr[h] * corr + pv[:, :LANES]
            l_scr[h] = l_scr[h] * corr + pv[:, LANES:]
            m_scr[h] = m_next

    def body(j, carry):
        block(j, False)
        return carry

    lax.fori_loop(0, qi, body, 0)
    block(qi, True)

    lp = lam_ref[...]
    lam_init = 0.8 - 0.6 * math.exp(-0.3 * layer)
    lam = (jnp.exp(jnp.sum(lp[0:1] * lp[1:2], axis=-1, keepdims=True))
           - jnp.exp(jnp.sum(lp[2:3] * lp[3:4], axis=-1, keepdims=True)) + lam_init)
    nw = nw_ref[...]
    for h in range(DIFF_HEADS):
        o12 = acc_scr[h] / l_scr[h]
        o = o12[:t] - lam * o12[t:]
        o_ref[0, :, h * LANES:(h + 1) * LANES] = (_rms_norm(o, nw) * (1.0 - lam_init)).astype(o_ref.dtype)


def _diff_attn(qa, ka, va, lam_p, norm_w, layer):
    b, t, width = qa.shape
    tq = ATTN_TILE
    scr = pltpu.VMEM((DIFF_HEADS, 2 * tq, LANES), F32)
    return pl.pallas_call(
        functools.partial(_diff_attn_kernel, layer=layer),
        grid=(b, t // tq),
        in_specs=[pl.BlockSpec((1, tq, width), lambda bi, qi: (bi, qi, 0)),
                  pl.BlockSpec((1, t, width), lambda bi, qi: (bi, 0, 0)),
                  pl.BlockSpec((1, t, width), lambda bi, qi: (bi, 0, 0)),
                  _const_spec(lam_p.shape), _const_spec(norm_w.shape)],
        out_specs=pl.BlockSpec((1, tq, width), lambda bi, qi: (bi, qi, 0)),
        out_shape=jax.ShapeDtypeStruct((b, t, width), BF16),
        scratch_shapes=[scr, scr, scr],
        compiler_params=_params(("parallel", "arbitrary")),
        name="diff_attn",
    )(qa, ka, va, lam_p, norm_w)


_N_LEVELS = int(math.log2(CHUNK))
ROBUST_UNIT = SUBLANES


class _UnitLayout:
    def __init__(self, unit):
        self.unit = unit
        self.level0 = int(math.log2(unit))
        self.n_units = CHUNK // unit
        self.row_p = max(_N_LEVELS - self.level0 - 1, 0) * self.n_units
        self.row_q = self.row_p + self.n_units
        self.row_last = self.row_q + self.n_units
        self.packed_rows = -(-(self.row_last + 1) // 16) * 16

    def factor_row(self, li):
        return (li - self.level0 - 1) * self.n_units


def _level_table(lay, fast):
    i = np.arange(CHUNK)[:, None]
    j = np.arange(CHUNK)[None, :]
    lvl = np.where(i > j, np.floor(np.log2(np.maximum(np.bitwise_xor(i, j), 1))), -1.0)
    if fast:
        lvl = np.where(i >= j, np.maximum(lvl, lay.level0 - 1), -1.0)
    return lvl.astype(np.int32)


def _gate_sum_matrix(lay):
    c, nu, u = CHUNK, lay.n_units, lay.unit
    m = np.zeros((c + lay.packed_rows, c), np.float32)
    m[:c] = np.tril(np.ones((c, c), np.float32))
    for li in range(lay.level0 + 1, _N_LEVELS):
        half = (1 << li) // u
        for unit in range(nu):
            first = (unit // (2 * half)) * 2 * half
            boundary = u * (first + half)
            if unit - first >= half:
                m[c + lay.factor_row(li) + unit, boundary:u * unit] = 1.0
            else:
                m[c + lay.factor_row(li) + unit, u * unit + u:boundary] = 1.0
    for unit in range(nu):
        m[c + lay.row_p + unit, :u * unit] = 1.0
        m[c + lay.row_q + unit, u * unit + u:] = 1.0
    m[c + lay.row_last, :] = 1.0
    return np.concatenate([m, m], axis=1)


def _unit_rows(ref, blk, first_row, stride, lay):
    parts = [jnp.broadcast_to(ref[blk, pl.ds(first_row + stride * unit, 1), :], (lay.unit, LANES))
             for unit in range(lay.n_units)]
    return parts[0] if len(parts) == 1 else jnp.concatenate(parts, axis=0)


def _gla_kernel(q_ref, k_ref, v_ref, g_ref, og_ref, nw_ref, gsum_r_ref, lvl_r_ref, gsum_f_ref, lvl_f_ref,
                o_ref, st_scr, cum_scr, fac_scr, *, dk, n_heads, fast_unit):
    heads_per_block = LANES // dk
    n_blocks = n_heads // heads_per_block
    c = CHUNK
    sl = SUBLANES
    pad = SUBLANES

    @pl.when(pl.program_id(1) == 0)
    def _():
        st_scr[...] = jnp.zeros(st_scr.shape, F32)

    cum_scr[:, 0:pad, :] = jnp.zeros((n_blocks, pad, LANES), F32)
    nw = nw_ref[...]
    lane = lax.broadcasted_iota(jnp.int32, (c, LANES), 1)
    sub = lax.broadcasted_iota(jnp.int32, (sl, LANES), 0)

    def sweep(fast):
        lay = _UnitLayout(fast_unit if fast else ROBUST_UNIT)
        gsum = (gsum_f_ref if fast else gsum_r_ref)[...]
        lvl = (lvl_f_ref if fast else lvl_r_ref)[...]

        def chunk(ci, carry):
            rows = pl.ds(pl.multiple_of(ci * c, c), c)
            for blk in range(n_blocks):
                lanes = slice(blk * LANES, (blk + 1) * LANES)
                g = g_ref[0, rows, lanes]
                kb = k_ref[0, rows, lanes]
                qf = q_ref[0, rows, lanes].astype(F32)
                kf = kb.astype(F32)
                g_hi, g_mid, _ = _split_bf16(g)
                sums = _dot(gsum, jnp.concatenate([g_hi, g_mid], axis=0))
                cum = sums[:c]
                cum_scr[blk, pad:pad + c, :] = cum
                fac_scr[blk, 0:lay.packed_rows, :] = jnp.exp(sums[c:])

                before = cum - _unit_rows(cum_scr, blk, pad - 1, lay.unit, lay)
                q_u = qf * jnp.exp(before)
                k_u = kf * jnp.exp(_unit_rows(cum_scr, blk, pad + lay.unit - 1, lay.unit, lay) - cum)
                terms = []
                if fast:
                    terms.append((lay.level0 - 1, q_u.astype(BF16), (kf * jnp.exp(-before)).astype(BF16)))
                else:
                    terms.append((0, (qf * jnp.exp(g)).astype(BF16), kb))
                    m2 = jnp.concatenate(
                        [jnp.where(sub < 4,
                                   jnp.broadcast_to(cum_scr[blk, pl.ds(pad + sl * t + 1, 1), :], (sl, LANES)),
                                   jnp.broadcast_to(cum_scr[blk, pl.ds(pad + sl * t + 5, 1), :], (sl, LANES)))
                         for t in range(lay.n_units)], axis=0)
                    for li, m in ((1, m2), (2, _unit_rows(cum_scr, blk, pad + 3, sl, lay))):
                        e = jnp.exp(-jnp.abs(cum - m))
                        terms.append((li, (qf * e).astype(BF16), (kf * e).astype(BF16)))
                if lay.level0 < _N_LEVELS:
                    terms.append((lay.level0, q_u.astype(BF16), k_u.astype(BF16)))
                for li in range(lay.level0 + 1, _N_LEVELS):
                    f = _unit_rows(fac_scr, blk, lay.factor_row(li), 1, lay)
                    terms.append((li, (q_u * f).astype(BF16), (k_u * f).astype(BF16)))
                q_in = (q_u * _unit_rows(fac_scr, blk, lay.row_p, 1, lay)).astype(BF16)
                k_out = (k_u * _unit_rows(fac_scr, blk, lay.row_q, 1, lay)).astype(BF16)
                decay = fac_scr[blk, pl.ds(lay.row_last, 1), :]
                qk = None if fast else qf * kf

                for hb in range(heads_per_block):
                    h = blk * heads_per_block + hb
                    in_head = (lane >= hb * dk) & (lane < (hb + 1) * dk)
                    sel = (lambda a: a) if heads_per_block == 1 else (
                        lambda a: jnp.where(in_head, a, jnp.zeros_like(a)))
                    hl = slice(h * LANES, (h + 1) * LANES)
                    vh = v_ref[0, rows, hl]
                    scores = jnp.zeros((c, c), F32)
                    for li, q_l, k_l in terms:
                        scores = jnp.where(lvl == li, _dot_nt(sel(q_l), k_l), scores)
                    st = st_scr[h]
                    o = _dot(scores.astype(BF16), vh) + _dot_nt(sel(q_in), st.astype(BF16))
                    if not fast:
                        o = o + jnp.sum(sel(qk), axis=-1, keepdims=True) * vh.astype(F32)
                    st_scr[h] = st * decay + _dot_tn(vh, k_out)
                    og = og_ref[0, rows, hl].astype(F32)
                    o_ref[0, rows, hl] = (_rms_norm(o, nw) * og).astype(o_ref.dtype)
            return carry

        lax.fori_loop(0, REC_ROWS // c, chunk, 0, unroll=2)

    worst = None
    for r0 in range(0, REC_ROWS, fast_unit):
        tot = jnp.sum(g_ref[0, r0:r0 + fast_unit, :], axis=0, keepdims=True)
        worst = tot if worst is None else jnp.minimum(worst, tot)
    mild = jnp.min(worst) >= -FAST_PATH_MAX_DECAY

    @pl.when(mild)
    def _():
        sweep(True)

    @pl.when(jnp.logical_not(mild))
    def _():
        sweep(False)


def _gated_linear_attention(q, k, v, g, og, norm_w, dk, n_heads, fast_unit, name):
    b, t, hk = q.shape
    hv = v.shape[-1]
    rows = REC_ROWS
    n_blocks = hk // LANES
    lay_r, lay_f = _UnitLayout(ROBUST_UNIT), _UnitLayout(fast_unit)
    consts = [norm_w,
              jnp.asarray(_gate_sum_matrix(lay_r), BF16), jnp.asarray(_level_table(lay_r, False)),
              jnp.asarray(_gate_sum_matrix(lay_f), BF16), jnp.asarray(_level_table(lay_f, True))]
    seq = lambda width: pl.BlockSpec((1, rows, width), lambda bi, ci: (bi, ci, 0))
    return pl.pallas_call(
        functools.partial(_gla_kernel, dk=dk, n_heads=n_heads, fast_unit=fast_unit),
        grid=(b, t // rows),
        in_specs=[seq(hk), seq(hk), seq(hv), seq(hk), seq(hv)] + [_const_spec(a.shape) for a in consts],
        out_specs=seq(hv),
        out_shape=jax.ShapeDtypeStruct((b, t, hv), BF16),
        scratch_shapes=[pltpu.VMEM((n_heads, LANES, LANES), F32),
                        pltpu.VMEM((n_blocks, SUBLANES + CHUNK, LANES), F32),
                        pltpu.VMEM((n_blocks, max(lay_r.packed_rows, lay_f.packed_rows), LANES), F32)],
        compiler_params=_params(("parallel", "arbitrary")),
        name=name,
    )(q, k, v, g, og, *consts)


def _merge_kernel(x_ref, oa_ref, ob_ref, oc_ref, wmg_ref, wbr_ref, wout_ref, lnw_ref, lnb_ref, o_ref):
    for r0 in range(0, x_ref.shape[0], SUB_ROWS):
        rows = slice(r0, r0 + SUB_ROWS)
        x = x_ref[rows, :]
        xb = x.astype(BF16)
        mixed = None
        for n, br_ref in enumerate((oa_ref, ob_ref, oc_ref)):
            gate = _sigmoid(_dot(xb, wmg_ref[:, n * D_MODEL:(n + 1) * D_MODEL]))
            y = gate * _dot(br_ref[rows, :], wbr_ref[n])
            mixed = y if mixed is None else mixed + y
        mix = _dot(mixed.astype(BF16), wout_ref[...])
        o_ref[rows, :] = _layer_norm(ALPHA * x + mix, lnw_ref[...], lnb_ref[...])


def _merge(x2d, oa, ob, oc, wmg, wbr, wout, lnw, lnb):
    n = x2d.shape[0]
    tm = min(ROW_TILE, n)
    row = lambda width: pl.BlockSpec((tm, width), lambda i: (i, 0))
    return pl.pallas_call(
        _merge_kernel,
        grid=(n // tm,),
        in_specs=[row(D_MODEL), row(BRANCH_WIDTH), row(BRANCH_WIDTH), row(BRANCH_WIDTH),
                  _const_spec(wmg.shape), _const_spec(wbr.shape), _const_spec(wout.shape),
                  _const_spec(lnw.shape), _const_spec(lnb.shape)],
        out_specs=row(D_MODEL),
        out_shape=jax.ShapeDtypeStruct((n, D_MODEL), F32),
        compiler_params=_params(("parallel",)),
        name="merge",
    )(x2d, oa, ob, oc, wmg, wbr, wout, lnw, lnb)


def _mlp_kernel(x_ref, wup_ref, wdn_ref, lnw_ref, lnb_ref, o_ref):
    for r0 in range(0, x_ref.shape[0], SUB_ROWS):
        rows = slice(r0, r0 + SUB_ROWS)
        x = x_ref[rows, :]
        xb = x.astype(BF16)
        acc = None
        for f0 in range(0, D_FF, D_MODEL):
            hmid = jnp.maximum(_dot(xb, wup_ref[:, f0:f0 + D_MODEL]), 0.0)
            part = _dot((hmid * hmid).astype(BF16), wdn_ref[f0:f0 + D_MODEL, :])
            acc = part if acc is None else acc + part
        o_ref[rows, :] = _layer_norm(ALPHA * x + acc, lnw_ref[...], lnb_ref[...])


def _mlp(x2d, wup, wdn, lnw, lnb):
    n = x2d.shape[0]
    tm = min(ROW_TILE, n)
    row = pl.BlockSpec((tm, D_MODEL), lambda i: (i, 0))
    return pl.pallas_call(
        _mlp_kernel,
        grid=(n // tm,),
        in_specs=[row, _const_spec(wup.shape), _const_spec(wdn.shape),
                  _const_spec(lnw.shape), _const_spec(lnb.shape)],
        out_specs=row,
        out_shape=jax.ShapeDtypeStruct((n, D_MODEL), F32),
        compiler_params=_params(("parallel",)),
        name="mlp",
    )(x2d, wup, wdn, lnw, lnb)


def _layer(x2d, b, t, layer, w_in, gla_w_gate, gla_b_gate, diff_lambda, diff_norm_w, gla_norm_w,
           hgrn_norm_w, hgrn_lb, w_branch, w_out, ln1_w, ln1_b, w_up, w_down, ln2_w, ln2_b):
    n = b * t
    w = w_in[layer]
    wa = w[:, _A_Q:_B_Q].astype(BF16)
    wb = jnp.concatenate([w[:, _B_Q:_B_R], w[:, _B_G:_C_Q]], axis=1).astype(BF16)
    wr = jnp.pad(w[:, _B_R:_B_G], ((0, 0), (0, LANES - GLA_GATE_RANK))).astype(BF16)
    wc = w[:, _C_Q:_MG].astype(BF16)
    wmg = w[:, _MG:].astype(BF16)
    gw = jnp.pad(gla_w_gate[layer], ((0, LANES - GLA_GATE_RANK), (0, 0)))
    gb = gla_b_gate[layer][None, :]

    (qa, ka, va, qg, kg, vg, lg, sgo, qh, lf, kh, ih, sho) = _in_proj(
        x2d, wa, wb, wr, wc, gw, gb, hgrn_lb, layer)
    seq = lambda a: a.reshape(b, t, a.shape[-1])

    oa = _diff_attn(seq(qa), seq(ka), seq(va), diff_lambda[layer], diff_norm_w[layer][None, :], layer)
    ob = _gated_linear_attention(seq(qg), seq(kg), seq(vg), seq(lg), seq(sgo),
                                 gla_norm_w[layer][None, :], GLA_DK, GLA_HEADS, GLA_FAST_UNIT, "gla")
    oc = _gated_linear_attention(seq(qh), seq(kh), seq(ih), seq(lf), seq(sho),
                                 hgrn_norm_w[layer][None, :], HGRN_EXPAND, HGRN_HEADS, HGRN_FAST_UNIT, "hgrn")

    flat = lambda a: a.reshape(n, a.shape[-1])
    x1 = _merge(x2d, flat(oa), flat(ob), flat(oc), wmg, w_branch[layer].astype(BF16),
                w_out[layer].astype(BF16), ln1_w[layer][None, :], ln1_b[layer][None, :])
    return _mlp(x1, w_up[layer].astype(BF16), w_down[layer].astype(BF16),
                ln2_w[layer][None, :], ln2_b[layer][None, :])


def kernel(x, w_in, gla_w_gate, gla_b_gate, diff_lambda, diff_norm_w, gla_norm_w, hgrn_norm_w, hgrn_lb,
           w_branch, w_out, ln1_w, ln1_b, w_up, w_down, ln2_w, ln2_b):
    b, t, d = x.shape
    assert d == D_MODEL and w_in.shape == (DEPTH, D_MODEL, IN_WIDTH)
    assert t % REC_ROWS == 0 and t % ATTN_TILE == 0 and (b * t) % ROW_TILE == 0 and ROW_TILE % SUB_ROWS == 0
    x2d = x.reshape(b * t, d)
    for layer in range(DEPTH):
        x2d = _layer(x2d, b, t, layer, w_in, gla_w_gate, gla_b_gate, diff_lambda, diff_norm_w,
                     gla_norm_w, hgrn_norm_w, hgrn_lb, w_branch, w_out, ln1_w, ln1_b,
                     w_up, w_down, ln2_w, ln2_b)
    return x2d.reshape(b, t, d)
```

```python
import functools
import math

import numpy as np
import jax
import jax.numpy as jnp
from jax import lax
from jax.experimental import pallas as pl
from jax.experimental.pallas import tpu as pltpu

D_MODEL = 1024
DEPTH = 2
DIFF_HEADS = 4
DIFF_QK_DIM = 64
DIFF_V_DIM = 128
GLA_HEADS = 4
GLA_DK = 64
GLA_DV = 128
GLA_GATE_RANK = 16
GLA_GATE_NORM = 16.0
HGRN_HEADS = 4
HGRN_EXPAND = 128
HGRN_DV = 128
N_BRANCHES = 3
BRANCH_WIDTH = 512
D_FF = 4 * D_MODEL
ALPHA = (2 * DEPTH) ** 0.25
LN_EPS = 1e-5
MASK_VALUE = -1e30
LB_FLOOR = 1e-30

LANES = 128
SUBLANES = 8
VMEM_LIMIT_BYTES = 56 * 1024 * 1024

ROW_TILE = 512
SUB_ROWS = 256
ATTN_TILE = 512
CHUNK = 128
REC_ROWS = 1024
FAST_PATH_MAX_DECAY = 60.0
GLA_FAST_UNIT = 128
HGRN_FAST_UNIT = 32

F32 = jnp.float32
BF16 = jnp.bfloat16

_A_Q, _A_K, _A_V = 0, 512, 1024
_B_Q, _B_K, _B_V, _B_R, _B_G = 1536, 1792, 2048, 2560, 2576
_C_Q, _C_F, _C_I, _C_G = 3088, 3600, 4112, 4624
_MG = 5136
IN_WIDTH = _MG + N_BRANCHES * D_MODEL


def _dot(a, b):
    return jnp.dot(a, b, preferred_element_type=F32)


def _dot_nt(a, b):
    return lax.dot_general(a, b, (((1,), (1,)), ((), ())), preferred_element_type=F32)


def _dot_tn(a, b):
    return lax.dot_general(a, b, (((0,), (0,)), ((), ())), preferred_element_type=F32)


def _split_bf16(a):
    hi = a.astype(BF16)
    r = a - hi.astype(F32)
    mid = r.astype(BF16)
    lo = (r - mid.astype(F32)).astype(BF16)
    return hi, mid, lo


def _log_sigmoid(u):
    return jnp.minimum(u, 0.0) - jnp.log1p(jnp.exp(-jnp.abs(u)))


def _sigmoid(u):
    return 1.0 / (1.0 + jnp.exp(-u))


def _layer_norm(r, w, b):
    mu = jnp.mean(r, axis=-1, keepdims=True)
    c = r - mu
    var = jnp.mean(c * c, axis=-1, keepdims=True)
    return c * lax.rsqrt(var + LN_EPS) * w + b


def _rms_norm(o, w):
    return o * lax.rsqrt(jnp.mean(o * o, axis=-1, keepdims=True) + LN_EPS) * w


def _const_spec(shape):
    nd = len(shape)
    return pl.BlockSpec(shape, lambda *_: (0,) * nd, pipeline_mode=pl.Buffered(1))


def _params(semantics):
    return pltpu.CompilerParams(dimension_semantics=semantics, vmem_limit_bytes=VMEM_LIMIT_BYTES)


def _in_proj_kernel(x_ref, wa_ref, wb_ref, wr_ref, wc_ref, gw_ref, gb_ref, lb_ref,
                    qa_ref, ka_ref, va_ref, qg_ref, kg_ref, vg_ref, lg_ref, sgo_ref,
                    qh_ref, lf_ref, kh_ref, ih_ref, sho_ref, *, layer):
    lb_rows = [lb_ref[d:d + 1, :] for d in range(DEPTH)]
    mx = functools.reduce(jnp.maximum, lb_rows)
    es = [jnp.exp(r - mx) for r in lb_rows]
    tot = functools.reduce(jnp.add, es)
    soft = [e / tot for e in es]
    lb = functools.reduce(jnp.add, soft[:layer + 1]) - soft[0]
    lb_floored = jnp.maximum(lb, LB_FLOOR)
    one_minus_lb = 1.0 - lb
    w_hi, w_mid, _ = _split_bf16(gw_ref[...])

    for r0 in range(0, x_ref.shape[0], SUB_ROWS):
        rows = slice(r0, r0 + SUB_ROWS)
        xb = x_ref[rows, :].astype(BF16)

        def proj(w_ref, lo, width):
            return _dot(xb, w_ref[:, lo:lo + width])

        f = lb_floored + one_minus_lb * _sigmoid(proj(wc_ref, 512, 512))
        lf_ref[rows, :] = jnp.log(f)
        kh_ref[rows, :] = (1.0 - f).astype(BF16)
        qa_ref[rows, :] = (proj(wa_ref, 0, 512) * DIFF_QK_DIM ** -0.5).astype(BF16)
        ka_ref[rows, :] = proj(wa_ref, 512, 512).astype(BF16)
        gout = proj(wb_ref, 1024, 512)
        sgo_ref[rows, :] = (gout * _sigmoid(gout)).astype(BF16)
        va_ref[rows, :] = proj(wa_ref, 1024, 512).astype(BF16)
        qh_ref[rows, :] = proj(wc_ref, 0, 512).astype(BF16)
        hout = proj(wc_ref, 1536, 512)
        sho_ref[rows, :] = (hout * _sigmoid(hout)).astype(BF16)
        ih_ref[rows, :] = proj(wc_ref, 1024, 512).astype(BF16)
        vg_ref[rows, :] = proj(wb_ref, 512, 512).astype(BF16)
        glr = proj(wr_ref, 0, LANES)
        a_hi, a_mid, _ = _split_bf16(glr)
        u = _dot(a_hi, w_hi) + (_dot(a_hi, w_mid) + _dot(a_mid, w_hi)) + gb_ref[...]
        lg_ref[rows, :] = _log_sigmoid(u) * (1.0 / GLA_GATE_NORM)
        qg_ref[rows, :] = (proj(wb_ref, 0, 256) * GLA_DK ** -0.5).astype(BF16)
        kg_ref[rows, :] = proj(wb_ref, 256, 256).astype(BF16)


def _in_proj(x2d, wa, wb, wr, wc, gw, gb, lb, layer):
    n = x2d.shape[0]
    tm = min(ROW_TILE, n)
    row = lambda width: pl.BlockSpec((tm, width), lambda i: (i, 0))
    out_widths = [(512, BF16), (512, BF16), (512, BF16),
                  (256, BF16), (256, BF16), (512, BF16), (256, F32), (512, BF16),
                  (512, BF16), (512, F32), (512, BF16), (512, BF16), (512, BF16)]
    return pl.pallas_call(
        functools.partial(_in_proj_kernel, layer=layer),
        grid=(n // tm,),
        in_specs=[row(D_MODEL), _const_spec(wa.shape), _const_spec(wb.shape), _const_spec(wr.shape),
                  _const_spec(wc.shape), _const_spec(gw.shape), _const_spec(gb.shape), _const_spec(lb.shape)],
        out_specs=[row(w) for w, _ in out_widths],
        out_shape=[jax.ShapeDtypeStruct((n, w), dt) for w, dt in out_widths],
        compiler_params=_params(("parallel",)),
        name="in_proj",
    )(x2d, wa, wb, wr, wc, gw, gb, lb)


def _diff_attn_kernel(q_ref, k_ref, v_ref, lam_ref, nw_ref, o_ref, m_scr, l_scr, acc_scr, *, layer):
    t = ATTN_TILE
    qi = pl.program_id(1)

    lane = lax.broadcasted_iota(jnp.int32, (t, LANES), 1)
    qs = []
    for h in range(DIFF_HEADS):
        q = q_ref[0, :, h * LANES:(h + 1) * LANES]
        zero = jnp.zeros_like(q)
        qs.append(jnp.concatenate([jnp.where(lane < DIFF_QK_DIM, q, zero),
                                   jnp.where(lane >= DIFF_QK_DIM, q, zero)], axis=0))

    m_scr[...] = jnp.full(m_scr.shape, -jnp.inf, F32)
    l_scr[...] = jnp.zeros(l_scr.shape, F32)
    acc_scr[...] = jnp.zeros(acc_scr.shape, F32)
    col = lax.broadcasted_iota(jnp.int32, (1, t), 1)
    ones = jnp.ones((t, LANES), BF16)

    def block(j, masked):
        off = pl.multiple_of(j * t, t)
        rel = ((j - qi) * t + col).astype(F32)
        if masked:
            r = lax.broadcasted_iota(jnp.int32, (t, t), 0)
            c = lax.broadcasted_iota(jnp.int32, (t, t), 1)
            keep = jnp.concatenate([r >= c, r >= c], axis=0)
        for h in range(DIFF_HEADS):
            slope = 2.0 ** (-8.0 * (h + 1) / DIFF_HEADS)
            kb = k_ref[0, pl.ds(off, t), h * LANES:(h + 1) * LANES]
            vb = v_ref[0, pl.ds(off, t), h * LANES:(h + 1) * LANES]
            s = _dot_nt(qs[h], kb) + slope * rel
            if masked:
                s = jnp.where(keep, s, MASK_VALUE)
            m_prev = m_scr[h]
            m_next = jnp.maximum(m_prev, jnp.max(s, axis=-1, keepdims=True))
            corr = jnp.exp(m_prev - m_next)
            p = jnp.concatenate([jnp.exp(s[:, b0:b0 + LANES] - m_next) for b0 in range(0, t, LANES)],
                                axis=1).astype(BF16)
            pv = _dot(p, jnp.concatenate([vb, ones], axis=1))
            acc_scr[h] = acc_scr[h] * corr + pv[:, :LANES]
            l_scr[h] = l_scr[h] * corr + pv[:, LANES:]
            m_scr[h] = m_next

    def body(j, carry):
        block(j, False)
        return carry

    lax.fori_loop(0, qi, body, 0)
    block(qi, True)

    lp = lam_ref[...]
    lam_init = 0.8 - 0.6 * math.exp(-0.3 * layer)
    lam = (jnp.exp(jnp.sum(lp[0:1] * lp[1:2], axis=-1, keepdims=True))
           - jnp.exp(jnp.sum(lp[2:3] * lp[3:4], axis=-1, keepdims=True)) + lam_init)
    nw = nw_ref[...]
    for h in range(DIFF_HEADS):
        o12 = acc_scr[h] / l_scr[h]
        o = o12[:t] - lam * o12[t:]
        o_ref[0, :, h * LANES:(h + 1) * LANES] = (_rms_norm(o, nw) * (1.0 - lam_init)).astype(o_ref.dtype)


def _diff_attn(qa, ka, va, lam_p, norm_w, layer):
    b, t, width = qa.shape
    tq = ATTN_TILE
    scr = pltpu.VMEM((DIFF_HEADS, 2 * tq, LANES), F32)
    return pl.pallas_call(
        functools.partial(_diff_attn_kernel, layer=layer),
        grid=(b, t // tq),
        in_specs=[pl.BlockSpec((1, tq, width), lambda bi, qi: (bi, qi, 0)),
                  pl.BlockSpec((1, t, width), lambda bi, qi: (bi, 0, 0)),
                  pl.BlockSpec((1, t, width), lambda bi, qi: (bi, 0, 0)),
                  _const_spec(lam_p.shape), _const_spec(norm_w.shape)],
        out_specs=pl.BlockSpec((1, tq, width), lambda bi, qi: (bi, qi, 0)),
        out_shape=jax.ShapeDtypeStruct((b, t, width), BF16),
        scratch_shapes=[scr, scr, scr],
        compiler_params=_params(("parallel", "arbitrary")),
        name="diff_attn",
    )(qa, ka, va, lam_p, norm_w)


_N_LEVELS = int(math.log2(CHUNK))
ROBUST_UNIT = SUBLANES


class _UnitLayout:
    def __init__(self, unit):
        self.unit = unit
        self.level0 = int(math.log2(unit))
        self.n_units = CHUNK // unit
        self.row_p = max(_N_LEVELS - self.level0 - 1, 0) * self.n_units
        self.row_q = self.row_p + self.n_units
        self.row_last = self.row_q + self.n_units
        self.packed_rows = -(-(self.row_last + 1) // 16) * 16

    def factor_row(self, li):
        return (li - self.level0 - 1) * self.n_units


def _level_table(lay, fast):
    i = np.arange(CHUNK)[:, None]
    j = np.arange(CHUNK)[None, :]
    lvl = np.where(i > j, np.floor(np.log2(np.maximum(np.bitwise_xor(i, j), 1))), -1.0)
    if fast:
        lvl = np.where(i >= j, np.maximum(lvl, lay.level0 - 1), -1.0)
    return lvl.astype(np.int32)


def _gate_sum_matrix(lay):
    c, nu, u = CHUNK, lay.n_units, lay.unit
    m = np.zeros((c + lay.packed_rows, c), np.float32)
    m[:c] = np.tril(np.ones((c, c), np.float32))
    for li in range(lay.level0 + 1, _N_LEVELS):
        half = (1 << li) // u
        for unit in range(nu):
            first = (unit // (2 * half)) * 2 * half
            boundary = u * (first + half)
            if unit - first >= half:
                m[c + lay.factor_row(li) + unit, boundary:u * unit] = 1.0
            else:
                m[c + lay.factor_row(li) + unit, u * unit + u:boundary] = 1.0
    for unit in range(nu):
        m[c + lay.row_p + unit, :u * unit] = 1.0
        m[c + lay.row_q + unit, u * unit + u:] = 1.0
    m[c + lay.row_last, :] = 1.0
    return np.concatenate([m, m], axis=1)


def _unit_rows(ref, blk, first_row, stride, lay):
    parts = [jnp.broadcast_to(ref[blk, pl.ds(first_row + stride * unit, 1), :], (lay.unit, LANES))
             for unit in range(lay.n_units)]
    return parts[0] if len(parts) == 1 else jnp.concatenate(parts, axis=0)


def _gla_kernel(q_ref, k_ref, v_ref, g_ref, og_ref, nw_ref, gsum_r_ref, lvl_r_ref, gsum_f_ref, lvl_f_ref,
                o_ref, st_scr, cum_scr, fac_scr, *, dk, n_heads, fast_unit):
    heads_per_block = LANES // dk
    n_blocks = n_heads // heads_per_block
    c = CHUNK
    sl = SUBLANES
    pad = SUBLANES

    @pl.when(pl.program_id(1) == 0)
    def _():
        st_scr[...] = jnp.zeros(st_scr.shape, F32)

    cum_scr[:, 0:pad, :] = jnp.zeros((n_blocks, pad, LANES), F32)
    nw = nw_ref[...]
    lane = lax.broadcasted_iota(jnp.int32, (c, LANES), 1)
    sub = lax.broadcasted_iota(jnp.int32, (sl, LANES), 0)

    def sweep(fast):
        lay = _UnitLayout(fast_unit if fast else ROBUST_UNIT)
        gsum = (gsum_f_ref if fast else gsum_r_ref)[...]
        lvl = (lvl_f_ref if fast else lvl_r_ref)[...]

        def chunk(ci, carry):
            rows = pl.ds(pl.multiple_of(ci * c, c), c)
            for blk in range(n_blocks):
                lanes = slice(blk * LANES, (blk + 1) * LANES)
                g = g_ref[0, rows, lanes]
                kb = k_ref[0, rows, lanes]
                qf = q_ref[0, rows, lanes].astype(F32)
                kf = kb.astype(F32)
                g_hi, g_mid, _ = _split_bf16(g)
                sums = _dot(gsum, jnp.concatenate([g_hi, g_mid], axis=0))
                cum = sums[:c]
                cum_scr[blk, pad:pad + c, :] = cum
                fac_scr[blk, 0:lay.packed_rows, :] = jnp.exp(sums[c:])

                before = cum - _unit_rows(cum_scr, blk, pad - 1, lay.unit, lay)
                q_u = qf * jnp.exp(before)
                k_u = kf * jnp.exp(_unit_rows(cum_scr, blk, pad + lay.unit - 1, lay.unit, lay) - cum)
                terms = []
                if fast:
                    terms.append((lay.level0 - 1, q_u.astype(BF16), (kf * jnp.exp(-before)).astype(BF16)))
                else:
                    terms.append((0, (qf * jnp.exp(g)).astype(BF16), kb))
                    m2 = jnp.concatenate(
                        [jnp.where(sub < 4,
                                   jnp.broadcast_to(cum_scr[blk, pl.ds(pad + sl * t + 1, 1), :], (sl, LANES)),
                                   jnp.broadcast_to(cum_scr[blk, pl.ds(pad + sl * t + 5, 1), :], (sl, LANES)))
                         for t in range(lay.n_units)], axis=0)
                    for li, m in ((1, m2), (2, _unit_rows(cum_scr, blk, pad + 3, sl, lay))):
                        e = jnp.exp(-jnp.abs(cum - m))
                        terms.append((li, (qf * e).astype(BF16), (kf * e).astype(BF16)))
                if lay.level0 < _N_LEVELS:
                    terms.append((lay.level0, q_u.astype(BF16), k_u.astype(BF16)))
                for li in range(lay.level0 + 1, _N_LEVELS):
                    f = _unit_rows(fac_scr, blk, lay.factor_row(li), 1, lay)
                    terms.append((li, (q_u * f).astype(BF16), (k_u * f).astype(BF16)))
                q_in = (q_u * _unit_rows(fac_scr, blk, lay.row_p, 1, lay)).astype(BF16)
                k_out = (k_u * _unit_rows(fac_scr, blk, lay.row_q, 1, lay)).astype(BF16)
                decay = fac_scr[blk, pl.ds(lay.row_last, 1), :]
                qk = None if fast else qf * kf

                for hb in range(heads_per_block):
                    h = blk * heads_per_block + hb
                    in_head = (lane >= hb * dk) & (lane < (hb + 1) * dk)
                    sel = (lambda a: a) if heads_per_block == 1 else (
                        lambda a: jnp.where(in_head, a, jnp.zeros_like(a)))
                    hl = slice(h * LANES, (h + 1) * LANES)
                    vh = v_ref[0, rows, hl]
                    scores = jnp.zeros((c, c), F32)
                    for li, q_l, k_l in terms:
                        scores = jnp.where(lvl == li, _dot_nt(sel(q_l), k_l), scores)
                    st = st_scr[h]
                    o = _dot(scores.astype(BF16), vh) + _dot_nt(sel(q_in), st.astype(BF16))
                    if not fast:
                        o = o + jnp.sum(sel(qk), axis=-1, keepdims=True) * vh.astype(F32)
                    st_scr[h] = st * decay + _dot_tn(vh, k_out)
                    og = og_ref[0, rows, hl].astype(F32)
                    o_ref[0, rows, hl] = (_rms_norm(o, nw) * og).astype(o_ref.dtype)
            return carry

        lax.fori_loop(0, REC_ROWS // c, chunk, 0, unroll=REC_ROWS // c if fast else 2)

    worst = None
    for r0 in range(0, REC_ROWS, fast_unit):
        tot = jnp.sum(g_ref[0, r0:r0 + fast_unit, :], axis=0, keepdims=True)
        worst = tot if worst is None else jnp.minimum(worst, tot)
    mild = jnp.min(worst) >= -FAST_PATH_MAX_DECAY

    @pl.when(mild)
    def _():
        sweep(True)

    @pl.when(jnp.logical_not(mild))
    def _():
        sweep(False)


def _gated_linear_attention(q, k, v, g, og, norm_w, dk, n_heads, fast_unit, name):
    b, t, hk = q.shape
    hv = v.shape[-1]
    rows = REC_ROWS
    n_blocks = hk // LANES
    lay_r, lay_f = _UnitLayout(ROBUST_UNIT), _UnitLayout(fast_unit)
    consts = [norm_w,
              jnp.asarray(_gate_sum_matrix(lay_r), BF16), jnp.asarray(_level_table(lay_r, False)),
              jnp.asarray(_gate_sum_matrix(lay_f), BF16), jnp.asarray(_level_table(lay_f, True))]
    seq = lambda width: pl.BlockSpec((1, rows, width), lambda bi, ci: (bi, ci, 0))
    return pl.pallas_call(
        functools.partial(_gla_kernel, dk=dk, n_heads=n_heads, fast_unit=fast_unit),
        grid=(b, t // rows),
        in_specs=[seq(hk), seq(hk), seq(hv), seq(hk), seq(hv)] + [_const_spec(a.shape) for a in consts],
        out_specs=seq(hv),
        out_shape=jax.ShapeDtypeStruct((b, t, hv), BF16),
        scratch_shapes=[pltpu.VMEM((n_heads, LANES, LANES), F32),
                        pltpu.VMEM((n_blocks, SUBLANES + CHUNK, LANES), F32),
                        pltpu.VMEM((n_blocks, max(lay_r.packed_rows, lay_f.packed_rows), LANES), F32)],
        compiler_params=_params(("parallel", "arbitrary")),
        name=name,
    )(q, k, v, g, og, *consts)


def _merge_kernel(x_ref, oa_ref, ob_ref, oc_ref, wmg_ref, wbr_ref, wout_ref, lnw_ref, lnb_ref, o_ref):
    for r0 in range(0, x_ref.shape[0], SUB_ROWS):
        rows = slice(r0, r0 + SUB_ROWS)
        x = x_ref[rows, :]
        xb = x.astype(BF16)
        mixed = None
        for n, br_ref in enumerate((oa_ref, ob_ref, oc_ref)):
            gate = _sigmoid(_dot(xb, wmg_ref[:, n * D_MODEL:(n + 1) * D_MODEL]))
            y = gate * _dot(br_ref[rows, :], wbr_ref[n])
            mixed = y if mixed is None else mixed + y
        mix = _dot(mixed.astype(BF16), wout_ref[...])
        o_ref[rows, :] = _layer_norm(ALPHA * x + mix, lnw_ref[...], lnb_ref[...])


def _merge(x2d, oa, ob, oc, wmg, wbr, wout, lnw, lnb):
    n = x2d.shape[0]
    tm = min(ROW_TILE, n)
    row = lambda width: pl.BlockSpec((tm, width), lambda i: (i, 0))
    return pl.pallas_call(
        _merge_kernel,
        grid=(n // tm,),
        in_specs=[row(D_MODEL), row(BRANCH_WIDTH), row(BRANCH_WIDTH), row(BRANCH_WIDTH),
                  _const_spec(wmg.shape), _const_spec(wbr.shape), _const_spec(wout.shape),
                  _const_spec(lnw.shape), _const_spec(lnb.shape)],
        out_specs=row(D_MODEL),
        out_shape=jax.ShapeDtypeStruct((n, D_MODEL), F32),
        compiler_params=_params(("parallel",)),
        name="merge",
    )(x2d, oa, ob, oc, wmg, wbr, wout, lnw, lnb)


def _mlp_kernel(x_ref, wup_ref, wdn_ref, lnw_ref, lnb_ref, o_ref):
    for r0 in range(0, x_ref.shape[0], SUB_ROWS):
        rows = slice(r0, r0 + SUB_ROWS)
        x = x_ref[rows, :]
        xb = x.astype(BF16)
        acc = None
        for f0 in range(0, D_FF, D_MODEL):
            hmid = jnp.maximum(_dot(xb, wup_ref[:, f0:f0 + D_MODEL]), 0.0)
            part = _dot((hmid * hmid).astype(BF16), wdn_ref[f0:f0 + D_MODEL, :])
            acc = part if acc is None else acc + part
        o_ref[rows, :] = _layer_norm(ALPHA * x + acc, lnw_ref[...], lnb_ref[...])


def _mlp(x2d, wup, wdn, lnw, lnb):
    n = x2d.shape[0]
    tm = min(ROW_TILE, n)
    row = pl.BlockSpec((tm, D_MODEL), lambda i: (i, 0))
    return pl.pallas_call(
        _mlp_kernel,
        grid=(n // tm,),
        in_specs=[row, _const_spec(wup.shape), _const_spec(wdn.shape),
                  _const_spec(lnw.shape), _const_spec(lnb.shape)],
        out_specs=row,
        out_shape=jax.ShapeDtypeStruct((n, D_MODEL), F32),
        compiler_params=_params(("parallel",)),
        name="mlp",
    )(x2d, wup, wdn, lnw, lnb)


def _layer(x2d, b, t, layer, w_in, gla_w_gate, gla_b_gate, diff_lambda, diff_norm_w, gla_norm_w,
           hgrn_norm_w, hgrn_lb, w_branch, w_out, ln1_w, ln1_b, w_up, w_down, ln2_w, ln2_b):
    n = b * t
    w = w_in[layer]
    wa = w[:, _A_Q:_B_Q]
    wb = jnp.concatenate([w[:, _B_Q:_B_R], w[:, _B_G:_C_Q]], axis=1)
    wr = jnp.pad(w[:, _B_R:_B_G], ((0, 0), (0, LANES - GLA_GATE_RANK)))
    wc = w[:, _C_Q:_MG]
    wmg = w[:, _MG:]
    gw = jnp.pad(gla_w_gate[layer], ((0, LANES - GLA_GATE_RANK), (0, 0)))
    gb = gla_b_gate[layer][None, :]

    (qa, ka, va, qg, kg, vg, lg, sgo, qh, lf, kh, ih, sho) = _in_proj(
        x2d, wa, wb, wr, wc, gw, gb, hgrn_lb, layer)
    seq = lambda a: a.reshape(b, t, a.shape[-1])

    oa = _diff_attn(seq(qa), seq(ka), seq(va), diff_lambda[layer], diff_norm_w[layer][None, :], layer)
    ob = _gated_linear_attention(seq(qg), seq(kg), seq(vg), seq(lg), seq(sgo),
                                 gla_norm_w[layer][None, :], GLA_DK, GLA_HEADS, GLA_FAST_UNIT, "gla")
    oc = _gated_linear_attention(seq(qh), seq(kh), seq(ih), seq(lf), seq(sho),
                                 hgrn_norm_w[layer][None, :], HGRN_EXPAND, HGRN_HEADS, HGRN_FAST_UNIT, "hgrn")

    flat = lambda a: a.reshape(n, a.shape[-1])
    x1 = _merge(x2d, flat(oa), flat(ob), flat(oc), wmg, w_branch[layer], w_out[layer],
                ln1_w[layer][None, :], ln1_b[layer][None, :])
    return _mlp(x1, w_up[layer], w_down[layer], ln2_w[layer][None, :], ln2_b[layer][None, :])


def kernel(x, w_in, gla_w_gate, gla_b_gate, diff_lambda, diff_norm_w, gla_norm_w, hgrn_norm_w, hgrn_lb,
           w_branch, w_out, ln1_w, ln1_b, w_up, w_down, ln2_w, ln2_b):
    b, t, d = x.shape
    assert d == D_MODEL and w_in.shape == (DEPTH, D_MODEL, IN_WIDTH)
    assert t % REC_ROWS == 0 and t % ATTN_TILE == 0 and (b * t) % ROW_TILE == 0 and ROW_TILE % SUB_ROWS == 0
    x2d = x.reshape(b * t, d)
    w_in, w_branch, w_out, w_up, w_down = (a.astype(BF16) for a in (w_in, w_branch, w_out, w_up, w_down))
    for layer in range(DEPTH):
        x2d = _layer(x2d, b, t, layer, w_in, gla_w_gate, gla_b_gate, diff_lambda, diff_norm_w,
                     gla_norm_w, hgrn_norm_w, hgrn_lb, w_branch, w_out, ln1_w, ln1_b,
                     w_up, w_down, ln2_w, ln2_b)
    return x2d.reshape(b, t, d)
```

```python
import functools
import math

import numpy as np
import jax
import jax.numpy as jnp
from jax import lax
from jax.experimental import pallas as pl
from jax.experimental.pallas import tpu as pltpu

D_MODEL = 1024
DEPTH = 2
DIFF_HEADS = 4
DIFF_QK_DIM = 64
DIFF_V_DIM = 128
GLA_HEADS = 4
GLA_DK = 64
GLA_DV = 128
GLA_GATE_RANK = 16
GLA_GATE_NORM = 16.0
HGRN_HEADS = 4
HGRN_EXPAND = 128
HGRN_DV = 128
N_BRANCHES = 3
BRANCH_WIDTH = 512
D_FF = 4 * D_MODEL
ALPHA = (2 * DEPTH) ** 0.25
LN_EPS = 1e-5
MASK_VALUE = -1e30
LB_FLOOR = 1e-30

LANES = 128
SUBLANES = 8
VMEM_LIMIT_BYTES = 56 * 1024 * 1024

ROW_TILE = 1024
SUB_ROWS = 256
ATTN_TILE = 512
CHUNK = 128
REC_ROWS = 1024
FAST_PATH_MAX_DECAY = 60.0
SCRATCH_SLOTS = 3
GLA_FAST_UNIT = 128
HGRN_FAST_UNIT = 32

F32 = jnp.float32
BF16 = jnp.bfloat16

_A_Q, _A_K, _A_V = 0, 512, 1024
_B_Q, _B_K, _B_V, _B_R, _B_G = 1536, 1792, 2048, 2560, 2576
_C_Q, _C_F, _C_I, _C_G = 3088, 3600, 4112, 4624
_MG = 5136
IN_WIDTH = _MG + N_BRANCHES * D_MODEL


def _dot(a, b):
    return jnp.dot(a, b, preferred_element_type=F32)


def _dot_nt(a, b):
    return lax.dot_general(a, b, (((1,), (1,)), ((), ())), preferred_element_type=F32)


def _dot_tn(a, b):
    return lax.dot_general(a, b, (((0,), (0,)), ((), ())), preferred_element_type=F32)


def _split_bf16(a):
    hi = a.astype(BF16)
    r = a - hi.astype(F32)
    mid = r.astype(BF16)
    lo = (r - mid.astype(F32)).astype(BF16)
    return hi, mid, lo


def _log_sigmoid(u):
    return jnp.minimum(u, 0.0) - jnp.log1p(jnp.exp(-jnp.abs(u)))


def _sigmoid(u):
    return 1.0 / (1.0 + jnp.exp(-u))


def _layer_norm(r, w, b):
    mu = jnp.mean(r, axis=-1, keepdims=True)
    c = r - mu
    var = jnp.mean(c * c, axis=-1, keepdims=True)
    return c * lax.rsqrt(var + LN_EPS) * w + b


def _rms_norm(o, w):
    return o * lax.rsqrt(jnp.mean(o * o, axis=-1, keepdims=True) + LN_EPS) * w


def _const_spec(shape):
    nd = len(shape)
    return pl.BlockSpec(shape, lambda *_: (0,) * nd, pipeline_mode=pl.Buffered(1))


def _params(semantics):
    return pltpu.CompilerParams(dimension_semantics=semantics, vmem_limit_bytes=VMEM_LIMIT_BYTES)


def _in_proj_kernel(x_ref, wa_ref, wb_ref, wr_ref, wc_ref, gw_ref, gb_ref, lb_ref,
                    qa_ref, ka_ref, va_ref, qg_ref, kg_ref, vg_ref, lg_ref, sgo_ref,
                    qh_ref, lf_ref, kh_ref, ih_ref, sho_ref, *, layer):
    lb_rows = [lb_ref[d:d + 1, :] for d in range(DEPTH)]
    mx = functools.reduce(jnp.maximum, lb_rows)
    es = [jnp.exp(r - mx) for r in lb_rows]
    tot = functools.reduce(jnp.add, es)
    soft = [e / tot for e in es]
    lb = functools.reduce(jnp.add, soft[:layer + 1]) - soft[0]
    lb_floored = jnp.maximum(lb, LB_FLOOR)
    one_minus_lb = 1.0 - lb
    w_hi, w_mid, _ = _split_bf16(gw_ref[...])

    for r0 in range(0, x_ref.shape[0], SUB_ROWS):
        rows = slice(r0, r0 + SUB_ROWS)
        xb = x_ref[rows, :].astype(BF16)

        def proj(w_ref, lo, width):
            return _dot(xb, w_ref[:, lo:lo + width])

        f = lb_floored + one_minus_lb * _sigmoid(proj(wc_ref, 512, 512))
        lf_ref[rows, :] = jnp.log(f)
        kh_ref[rows, :] = (1.0 - f).astype(BF16)
        qa_ref[rows, :] = (proj(wa_ref, 0, 512) * DIFF_QK_DIM ** -0.5).astype(BF16)
        ka_ref[rows, :] = proj(wa_ref, 512, 512).astype(BF16)
        gout = proj(wb_ref, 1024, 512)
        sgo_ref[rows, :] = (gout * _sigmoid(gout)).astype(BF16)
        va_ref[rows, :] = proj(wa_ref, 1024, 512).astype(BF16)
        qh_ref[rows, :] = proj(wc_ref, 0, 512).astype(BF16)
        hout = proj(wc_ref, 1536, 512)
        sho_ref[rows, :] = (hout * _sigmoid(hout)).astype(BF16)
        ih_ref[rows, :] = proj(wc_ref, 1024, 512).astype(BF16)
        vg_ref[rows, :] = proj(wb_ref, 512, 512).astype(BF16)
        glr = proj(wr_ref, 0, LANES)
        a_hi, a_mid, _ = _split_bf16(glr)
        u = _dot(a_hi, w_hi) + (_dot(a_hi, w_mid) + _dot(a_mid, w_hi)) + gb_ref[...]
        lg_ref[rows, :] = _log_sigmoid(u) * (1.0 / GLA_GATE_NORM)
        qg_ref[rows, :] = (proj(wb_ref, 0, 256) * GLA_DK ** -0.5).astype(BF16)
        kg_ref[rows, :] = proj(wb_ref, 256, 256).astype(BF16)


def _in_proj(x2d, wa, wb, wr, wc, gw, gb, lb, layer):
    n = x2d.shape[0]
    tm = min(ROW_TILE, n)
    row = lambda width: pl.BlockSpec((tm, width), lambda i: (i, 0))
    out_widths = [(512, BF16), (512, BF16), (512, BF16),
                  (256, BF16), (256, BF16), (512, BF16), (256, F32), (512, BF16),
                  (512, BF16), (512, F32), (512, BF16), (512, BF16), (512, BF16)]
    return pl.pallas_call(
        functools.partial(_in_proj_kernel, layer=layer),
        grid=(n // tm,),
        in_specs=[row(D_MODEL), _const_spec(wa.shape), _const_spec(wb.shape), _const_spec(wr.shape),
                  _const_spec(wc.shape), _const_spec(gw.shape), _const_spec(gb.shape), _const_spec(lb.shape)],
        out_specs=[row(w) for w, _ in out_widths],
        out_shape=[jax.ShapeDtypeStruct((n, w), dt) for w, dt in out_widths],
        compiler_params=_params(("parallel",)),
        name="in_proj",
    )(x2d, wa, wb, wr, wc, gw, gb, lb)


def _diff_attn_kernel(q_ref, k_ref, v_ref, lam_ref, nw_ref, o_ref, m_scr, l_scr, acc_scr, *, layer):
    t = ATTN_TILE
    qi = pl.program_id(1)

    lane = lax.broadcasted_iota(jnp.int32, (t, LANES), 1)
    qs = []
    for h in range(DIFF_HEADS):
        q = q_ref[0, :, h * LANES:(h + 1) * LANES]
        zero = jnp.zeros_like(q)
        qs.append(jnp.concatenate([jnp.where(lane < DIFF_QK_DIM, q, zero),
                                   jnp.where(lane >= DIFF_QK_DIM, q, zero)], axis=0))

    m_scr[...] = jnp.full(m_scr.shape, -jnp.inf, F32)
    l_scr[...] = jnp.zeros(l_scr.shape, F32)
    acc_scr[...] = jnp.zeros(acc_scr.shape, F32)
    col = lax.broadcasted_iota(jnp.int32, (1, t), 1)
    ones = jnp.ones((t, LANES), BF16)

    def block(j, masked):
        off = pl.multiple_of(j * t, t)
        rel = ((j - qi) * t + col).astype(F32)
        if masked:
            r = lax.broadcasted_iota(jnp.int32, (t, t), 0)
            c = lax.broadcasted_iota(jnp.int32, (t, t), 1)
            keep = jnp.concatenate([r >= c, r >= c], axis=0)
        scores = [_dot_nt(qs[h], k_ref[0, pl.ds(off, t), h * LANES:(h + 1) * LANES])
                  for h in range(DIFF_HEADS)]
        probs = []
        for h, s in enumerate(scores):
            s = s + 2.0 ** (-8.0 * (h + 1) / DIFF_HEADS) * rel
            if masked:
                s = jnp.where(keep, s, MASK_VALUE)
            m_prev = m_scr[h]
            m_next = jnp.maximum(m_prev, jnp.max(s, axis=-1, keepdims=True))
            m_scr[h] = m_next
            p = jnp.concatenate([jnp.exp(s[:, b0:b0 + LANES] - m_next) for b0 in range(0, t, LANES)],
                                axis=1).astype(BF16)
            probs.append((p, jnp.exp(m_prev - m_next)))
        for h, (p, corr) in enumerate(probs):
            vb = v_ref[0, pl.ds(off, t), h * LANES:(h + 1) * LANES]
            pv = _dot(p, jnp.concatenate([vb, ones], axis=1))
            acc_scr[h] = acc_scr[h] * corr + pv[:, :LANES]
            l_scr[h] = l_scr[h] * corr + pv[:, LANES:]

    def body(j, carry):
        block(j, False)
        return carry

    lax.fori_loop(0, qi, body, 0)
    block(qi, True)

    lp = lam_ref[...]
    lam_init = 0.8 - 0.6 * math.exp(-0.3 * layer)
    lam = (jnp.exp(jnp.sum(lp[0:1] * lp[1:2], axis=-1, keepdims=True))
           - jnp.exp(jnp.sum(lp[2:3] * lp[3:4], axis=-1, keepdims=True)) + lam_init)
    nw = nw_ref[...]
    for h in range(DIFF_HEADS):
        o12 = acc_scr[h] / l_scr[h]
        o = o12[:t] - lam * o12[t:]
        o_ref[0, :, h * LANES:(h + 1) * LANES] = (_rms_norm(o, nw) * (1.0 - lam_init)).astype(o_ref.dtype)


def _diff_attn(qa, ka, va, lam_p, norm_w, layer):
    b, t, width = qa.shape
    tq = ATTN_TILE
    scr = pltpu.VMEM((DIFF_HEADS, 2 * tq, LANES), F32)
    return pl.pallas_call(
        functools.partial(_diff_attn_kernel, layer=layer),
        grid=(b, t // tq),
        in_specs=[pl.BlockSpec((1, tq, width), lambda bi, qi: (bi, qi, 0)),
                  pl.BlockSpec((1, t, width), lambda bi, qi: (bi, 0, 0)),
                  pl.BlockSpec((1, t, width), lambda bi, qi: (bi, 0, 0)),
                  _const_spec(lam_p.shape), _const_spec(norm_w.shape)],
        out_specs=pl.BlockSpec((1, tq, width), lambda bi, qi: (bi, qi, 0)),
        out_shape=jax.ShapeDtypeStruct((b, t, width), BF16),
        scratch_shapes=[scr, scr, scr],
        compiler_params=_params(("parallel", "arbitrary")),
        name="diff_attn",
    )(qa, ka, va, lam_p, norm_w)


_N_LEVELS = int(math.log2(CHUNK))
ROBUST_UNIT = SUBLANES


class _UnitLayout:
    def __init__(self, unit):
        self.unit = unit
        self.level0 = int(math.log2(unit))
        self.n_units = CHUNK // unit
        self.row_p = max(_N_LEVELS - self.level0 - 1, 0) * self.n_units
        self.row_q = self.row_p + self.n_units
        self.row_last = self.row_q + self.n_units
        self.packed_rows = -(-(self.row_last + 1) // 16) * 16

    def factor_row(self, li):
        return (li - self.level0 - 1) * self.n_units


def _level_table(lay, fast):
    i = np.arange(CHUNK)[:, None]
    j = np.arange(CHUNK)[None, :]
    lvl = np.where(i > j, np.floor(np.log2(np.maximum(np.bitwise_xor(i, j), 1))), -1.0)
    if fast:
        lvl = np.where(i >= j, np.maximum(lvl, lay.level0 - 1), -1.0)
    return lvl.astype(np.int32)


def _gate_sum_matrix(lay):
    c, nu, u = CHUNK, lay.n_units, lay.unit
    m = np.zeros((c + lay.packed_rows, c), np.float32)
    m[:c] = np.tril(np.ones((c, c), np.float32))
    for li in range(lay.level0 + 1, _N_LEVELS):
        half = (1 << li) // u
        for unit in range(nu):
            first = (unit // (2 * half)) * 2 * half
            boundary = u * (first + half)
            if unit - first >= half:
                m[c + lay.factor_row(li) + unit, boundary:u * unit] = 1.0
            else:
                m[c + lay.factor_row(li) + unit, u * unit + u:boundary] = 1.0
    for unit in range(nu):
        m[c + lay.row_p + unit, :u * unit] = 1.0
        m[c + lay.row_q + unit, u * unit + u:] = 1.0
    m[c + lay.row_last, :] = 1.0
    return np.concatenate([m, m], axis=1)


def _unit_rows(ref, blk, first_row, stride, lay):
    parts = [jnp.broadcast_to(ref[blk, pl.ds(first_row + stride * unit, 1), :], (lay.unit, LANES))
             for unit in range(lay.n_units)]
    return parts[0] if len(parts) == 1 else jnp.concatenate(parts, axis=0)


def _gla_kernel(q_ref, k_ref, v_ref, g_ref, og_ref, nw_ref, gsum_r_ref, lvl_r_ref, gsum_f_ref, lvl_f_ref,
                o_ref, st_scr, cum_scr, fac_scr, *, dk, n_heads, fast_unit):
    heads_per_block = LANES // dk
    n_blocks = n_heads // heads_per_block
    c = CHUNK
    sl = SUBLANES
    pad = SUBLANES

    @pl.when(pl.program_id(1) == 0)
    def _():
        st_scr[...] = jnp.zeros(st_scr.shape, F32)

    cum_scr[:, 0:pad, :] = jnp.zeros((SCRATCH_SLOTS * n_blocks, pad, LANES), F32)
    nw = nw_ref[...]
    lane = lax.broadcasted_iota(jnp.int32, (c, LANES), 1)
    sub = lax.broadcasted_iota(jnp.int32, (sl, LANES), 0)

    def sweep(fast):
        lay = _UnitLayout(fast_unit if fast else ROBUST_UNIT)
        gsum = (gsum_f_ref if fast else gsum_r_ref)[...]
        lvl = (lvl_f_ref if fast else lvl_r_ref)[...]

        def gate_sums(rows, slot):
            loaded = []
            for blk in range(n_blocks):
                lanes = slice(blk * LANES, (blk + 1) * LANES)
                g = g_ref[0, rows, lanes]
                kb = k_ref[0, rows, lanes]
                qf = q_ref[0, rows, lanes].astype(F32)
                g_hi, g_mid, _ = _split_bf16(g)
                sums = _dot(gsum, jnp.concatenate([g_hi, g_mid], axis=0))
                cum_scr[slot * n_blocks + blk, pad:pad + c, :] = sums[:c]
                fac_scr[slot * n_blocks + blk, 0:lay.packed_rows, :] = jnp.exp(sums[c:])
                loaded.append((g, kb, qf, sums[:c]))
            return loaded

        def operands(loaded, slot):
            out = []
            for lane_blk, (g, kb, qf, cum) in enumerate(loaded):
                blk = slot * n_blocks + lane_blk
                kf = kb.astype(F32)
                before = cum - _unit_rows(cum_scr, blk, pad - 1, lay.unit, lay)
                q_u = qf * jnp.exp(before)
                k_u = kf * jnp.exp(_unit_rows(cum_scr, blk, pad + lay.unit - 1, lay.unit, lay) - cum)
                terms = []
                if fast:
                    terms.append((lay.level0 - 1, q_u.astype(BF16), (kf * jnp.exp(-before)).astype(BF16)))
                else:
                    terms.append((0, (qf * jnp.exp(g)).astype(BF16), kb))
                    m2 = jnp.concatenate(
                        [jnp.where(sub < 4,
                                   jnp.broadcast_to(cum_scr[blk, pl.ds(pad + sl * t + 1, 1), :], (sl, LANES)),
                                   jnp.broadcast_to(cum_scr[blk, pl.ds(pad + sl * t + 5, 1), :], (sl, LANES)))
                         for t in range(lay.n_units)], axis=0)
                    for li, m in ((1, m2), (2, _unit_rows(cum_scr, blk, pad + 3, sl, lay))):
                        e = jnp.exp(-jnp.abs(cum - m))
                        terms.append((li, (qf * e).astype(BF16), (kf * e).astype(BF16)))
                if lay.level0 < _N_LEVELS:
                    terms.append((lay.level0, q_u.astype(BF16), k_u.astype(BF16)))
                for li in range(lay.level0 + 1, _N_LEVELS):
                    f = _unit_rows(fac_scr, blk, lay.factor_row(li), 1, lay)
                    terms.append((li, (q_u * f).astype(BF16), (k_u * f).astype(BF16)))
                q_in = (q_u * _unit_rows(fac_scr, blk, lay.row_p, 1, lay)).astype(BF16)
                k_out = (k_u * _unit_rows(fac_scr, blk, lay.row_q, 1, lay)).astype(BF16)
                decay = fac_scr[blk, pl.ds(lay.row_last, 1), :]
                qk = None if fast else qf * kf
                out.append((terms, q_in, k_out, decay, qk))
            return out

        def score_parts(ops):
            heads = []
            for blk, (terms, q_in, k_out, decay, qk) in enumerate(ops):
                for hb in range(heads_per_block):
                    in_head = (lane >= hb * dk) & (lane < (hb + 1) * dk)
                    sel = (lambda a: a) if heads_per_block == 1 else (
                        lambda a, in_head=in_head: jnp.where(in_head, a, jnp.zeros_like(a)))
                    parts = [(li, _dot_nt(sel(q_l), k_l)) for li, q_l, k_l in terms]
                    heads.append((blk * heads_per_block + hb, sel, parts, q_in, k_out, decay, qk))
            return heads

        def outputs(rows, heads):
            outs = []
            for h, sel, parts, q_in, k_out, decay, qk in heads:
                vh = v_ref[0, rows, h * LANES:(h + 1) * LANES]
                scores = jnp.zeros((c, c), F32)
                for li, part in parts:
                    scores = jnp.where(lvl == li, part, scores)
                st = st_scr[h]
                o = _dot(scores.astype(BF16), vh) + _dot_nt(sel(q_in), st.astype(BF16))
                if not fast:
                    o = o + jnp.sum(sel(qk), axis=-1, keepdims=True) * vh.astype(F32)
                outs.append((h, vh, st, o, k_out, decay))

            for h, vh, st, o, k_out, decay in outs:
                hl = slice(h * LANES, (h + 1) * LANES)
                st_scr[h] = st * decay + _dot_tn(vh, k_out)
                og = og_ref[0, rows, hl].astype(F32)
                o_ref[0, rows, hl] = (_rms_norm(o, nw) * og).astype(o_ref.dtype)

        n_chunks = REC_ROWS // c
        if fast:
            rows_of = lambda ci: slice(ci * c, (ci + 1) * c)
            loaded = {ci: gate_sums(rows_of(ci), ci % SCRATCH_SLOTS) for ci in range(min(2, n_chunks))}
            ops = {0: operands(loaded.pop(0), 0)}
            for ci in range(n_chunks):
                heads = score_parts(ops.pop(ci))
                if ci + 2 < n_chunks:
                    loaded[ci + 2] = gate_sums(rows_of(ci + 2), (ci + 2) % SCRATCH_SLOTS)
                if ci + 1 < n_chunks:
                    ops[ci + 1] = operands(loaded.pop(ci + 1), (ci + 1) % SCRATCH_SLOTS)
                outputs(rows_of(ci), heads)
        else:
            def pair(p, carry):
                for half in range(2):
                    rows = pl.ds(pl.multiple_of((2 * p + half) * c, c), c)
                    outputs(rows, score_parts(operands(gate_sums(rows, half), half)))
                return carry

            lax.fori_loop(0, n_chunks // 2, pair, 0)

    worst = None
    for r0 in range(0, REC_ROWS, fast_unit):
        tot = jnp.sum(g_ref[0, r0:r0 + fast_unit, :], axis=0, keepdims=True)
        worst = tot if worst is None else jnp.minimum(worst, tot)
    mild = jnp.min(worst) >= -FAST_PATH_MAX_DECAY

    @pl.when(mild)
    def _():
        sweep(True)

    @pl.when(jnp.logical_not(mild))
    def _():
        sweep(False)


def _gated_linear_attention(q, k, v, g, og, norm_w, dk, n_heads, fast_unit, name):
    b, t, hk = q.shape
    hv = v.shape[-1]
    rows = REC_ROWS
    n_blocks = hk // LANES
    lay_r, lay_f = _UnitLayout(ROBUST_UNIT), _UnitLayout(fast_unit)
    consts = [norm_w,
              jnp.asarray(_gate_sum_matrix(lay_r), BF16), jnp.asarray(_level_table(lay_r, False)),
              jnp.asarray(_gate_sum_matrix(lay_f), BF16), jnp.asarray(_level_table(lay_f, True))]
    seq = lambda width: pl.BlockSpec((1, rows, width), lambda bi, ci: (bi, ci, 0))
    return pl.pallas_call(
        functools.partial(_gla_kernel, dk=dk, n_heads=n_heads, fast_unit=fast_unit),
        grid=(b, t // rows),
        in_specs=[seq(hk), seq(hk), seq(hv), seq(hk), seq(hv)] + [_const_spec(a.shape) for a in consts],
        out_specs=seq(hv),
        out_shape=jax.ShapeDtypeStruct((b, t, hv), BF16),
        scratch_shapes=[pltpu.VMEM((n_heads, LANES, LANES), F32),
                        pltpu.VMEM((SCRATCH_SLOTS * n_blocks, SUBLANES + CHUNK, LANES), F32),
                        pltpu.VMEM((SCRATCH_SLOTS * n_blocks, max(lay_r.packed_rows, lay_f.packed_rows), LANES), F32)],
        compiler_params=_params(("parallel", "arbitrary")),
        name=name,
    )(q, k, v, g, og, *consts)


def _merge_kernel(x_ref, oa_ref, ob_ref, oc_ref, wmg_ref, wbr_ref, wout_ref, lnw_ref, lnb_ref, o_ref):
    for r0 in range(0, x_ref.shape[0], SUB_ROWS):
        rows = slice(r0, r0 + SUB_ROWS)
        x = x_ref[rows, :]
        xb = x.astype(BF16)
        mixed = None
        for n, br_ref in enumerate((oa_ref, ob_ref, oc_ref)):
            gate = _sigmoid(_dot(xb, wmg_ref[:, n * D_MODEL:(n + 1) * D_MODEL]))
            y = gate * _dot(br_ref[rows, :], wbr_ref[n])
            mixed = y if mixed is None else mixed + y
        mix = _dot(mixed.astype(BF16), wout_ref[...])
        o_ref[rows, :] = _layer_norm(ALPHA * x + mix, lnw_ref[...], lnb_ref[...])


def _merge(x2d, oa, ob, oc, wmg, wbr, wout, lnw, lnb):
    n = x2d.shape[0]
    tm = min(ROW_TILE, n)
    row = lambda width: pl.BlockSpec((tm, width), lambda i: (i, 0))
    return pl.pallas_call(
        _merge_kernel,
        grid=(n // tm,),
        in_specs=[row(D_MODEL), row(BRANCH_WIDTH), row(BRANCH_WIDTH), row(BRANCH_WIDTH),
                  _const_spec(wmg.shape), _const_spec(wbr.shape), _const_spec(wout.shape),
                  _const_spec(lnw.shape), _const_spec(lnb.shape)],
        out_specs=row(D_MODEL),
        out_shape=jax.ShapeDtypeStruct((n, D_MODEL), F32),
        compiler_params=_params(("parallel",)),
        name="merge",
    )(x2d, oa, ob, oc, wmg, wbr, wout, lnw, lnb)


def _mlp_kernel(x_ref, wup_ref, wdn_ref, lnw_ref, lnb_ref, o_ref):
    for r0 in range(0, x_ref.shape[0], SUB_ROWS):
        rows = slice(r0, r0 + SUB_ROWS)
        x = x_ref[rows, :]
        xb = x.astype(BF16)
        acc = None
        for f0 in range(0, D_FF, D_MODEL):
            hmid = jnp.maximum(_dot(xb, wup_ref[:, f0:f0 + D_MODEL]), 0.0)
            part = _dot((hmid * hmid).astype(BF16), wdn_ref[f0:f0 + D_MODEL, :])
            acc = part if acc is None else acc + part
        o_ref[rows, :] = _layer_norm(ALPHA * x + acc, lnw_ref[...], lnb_ref[...])


def _mlp(x2d, wup, wdn, lnw, lnb):
    n = x2d.shape[0]
    tm = min(ROW_TILE, n)
    row = pl.BlockSpec((tm, D_MODEL), lambda i: (i, 0))
    return pl.pallas_call(
        _mlp_kernel,
        grid=(n // tm,),
        in_specs=[row, _const_spec(wup.shape), _const_spec(wdn.shape),
                  _const_spec(lnw.shape), _const_spec(lnb.shape)],
        out_specs=row,
        out_shape=jax.ShapeDtypeStruct((n, D_MODEL), F32),
        compiler_params=_params(("parallel",)),
        name="mlp",
    )(x2d, wup, wdn, lnw, lnb)


def _layer(x2d, b, t, layer, w_in, gla_w_gate, gla_b_gate, diff_lambda, diff_norm_w, gla_norm_w,
           hgrn_norm_w, hgrn_lb, w_branch, w_out, ln1_w, ln1_b, w_up, w_down, ln2_w, ln2_b):
    n = b * t
    w = w_in[layer]
    wa = w[:, _A_Q:_B_Q]
    wb = jnp.concatenate([w[:, _B_Q:_B_R], w[:, _B_G:_C_Q]], axis=1)
    wr = jnp.pad(w[:, _B_R:_B_G], ((0, 0), (0, LANES - GLA_GATE_RANK)))
    wc = w[:, _C_Q:_MG]
    wmg = w[:, _MG:]
    gw = jnp.pad(gla_w_gate[layer], ((0, LANES - GLA_GATE_RANK), (0, 0)))
    gb = gla_b_gate[layer][None, :]

    (qa, ka, va, qg, kg, vg, lg, sgo, qh, lf, kh, ih, sho) = _in_proj(
        x2d, wa, wb, wr, wc, gw, gb, hgrn_lb, layer)
    seq = lambda a: a.reshape(b, t, a.shape[-1])

    oa = _diff_attn(seq(qa), seq(ka), seq(va), diff_lambda[layer], diff_norm_w[layer][None, :], layer)
    ob = _gated_linear_attention(seq(qg), seq(kg), seq(vg), seq(lg), seq(sgo),
                                 gla_norm_w[layer][None, :], GLA_DK, GLA_HEADS, GLA_FAST_UNIT, "gla")
    oc = _gated_linear_attention(seq(qh), seq(kh), seq(ih), seq(lf), seq(sho),
                                 hgrn_norm_w[layer][None, :], HGRN_EXPAND, HGRN_HEADS, HGRN_FAST_UNIT, "hgrn")

    flat = lambda a: a.reshape(n, a.shape[-1])
    x1 = _merge(x2d, flat(oa), flat(ob), flat(oc), wmg, w_branch[layer], w_out[layer],
                ln1_w[layer][None, :], ln1_b[layer][None, :])
    return _mlp(x1, w_up[layer], w_down[layer], ln2_w[layer][None, :], ln2_b[layer][None, :])


def kernel(x, w_in, gla_w_gate, gla_b_gate, diff_lambda, diff_norm_w, gla_norm_w, hgrn_norm_w, hgrn_lb,
           w_branch, w_out, ln1_w, ln1_b, w_up, w_down, ln2_w, ln2_b):
    b, t, d = x.shape
    assert d == D_MODEL and w_in.shape == (DEPTH, D_MODEL, IN_WIDTH)
    assert t % REC_ROWS == 0 and t % ATTN_TILE == 0 and (b * t) % ROW_TILE == 0 and ROW_TILE % SUB_ROWS == 0
    x2d = x.reshape(b * t, d)
    w_in, w_branch, w_out, w_up, w_down = (a.astype(BF16) for a in (w_in, w_branch, w_out, w_up, w_down))
    for layer in range(DEPTH):
        x2d = _layer(x2d, b, t, layer, w_in, gla_w_gate, gla_b_gate, diff_lambda, diff_norm_w,
                     gla_norm_w, hgrn_norm_w, hgrn_lb, w_branch, w_out, ln1_w, ln1_b,
                     w_up, w_down, ln2_w, ln2_b)
    return x2d.reshape(b, t, d)
```

```python
import functools
import math

import numpy as np
import jax
import jax.numpy as jnp
from jax import lax
from jax.experimental import pallas as pl
from jax.experimental.pallas import tpu as pltpu

D_MODEL = 1024
DEPTH = 2
DIFF_HEADS = 4
DIFF_QK_DIM = 64
DIFF_V_DIM = 128
GLA_HEADS = 4
GLA_DK = 64
GLA_DV = 128
GLA_GATE_RANK = 16
GLA_GATE_NORM = 16.0
HGRN_HEADS = 4
HGRN_EXPAND = 128
HGRN_DV = 128
N_BRANCHES = 3
BRANCH_WIDTH = 512
D_FF = 4 * D_MODEL
ALPHA = (2 * DEPTH) ** 0.25
LN_EPS = 1e-5
MASK_VALUE = -1e30
LB_FLOOR = 1e-30

LANES = 128
SUBLANES = 8
VMEM_LIMIT_BYTES = 56 * 1024 * 1024

ROW_TILE = 1024
SUB_ROWS = 256
ATTN_TILE = 512
CHUNK = 128
REC_ROWS = 1024
FAST_PATH_MAX_DECAY = 60.0
SCRATCH_SLOTS = 3
GLA_FAST_UNIT = 128
HGRN_FAST_UNIT = 32

F32 = jnp.float32
BF16 = jnp.bfloat16

_A_Q, _A_K, _A_V = 0, 512, 1024
_B_Q, _B_K, _B_V, _B_R, _B_G = 1536, 1792, 2048, 2560, 2576
_C_Q, _C_F, _C_I, _C_G = 3088, 3600, 4112, 4624
_MG = 5136
IN_WIDTH = _MG + N_BRANCHES * D_MODEL
_T_BG, _T_CQ, _T_CF, _T_CI, _T_CG, _T_MG = (o - _B_G for o in (_B_G, _C_Q, _C_F, _C_I, _C_G, _MG))


def _dot(a, b):
    return jnp.dot(a, b, preferred_element_type=F32)


def _dot_nt(a, b):
    return lax.dot_general(a, b, (((1,), (1,)), ((), ())), preferred_element_type=F32)


def _dot_tn(a, b):
    return lax.dot_general(a, b, (((0,), (0,)), ((), ())), preferred_element_type=F32)


def _split_bf16(a):
    hi = a.astype(BF16)
    r = a - hi.astype(F32)
    mid = r.astype(BF16)
    lo = (r - mid.astype(F32)).astype(BF16)
    return hi, mid, lo


def _log_sigmoid(u):
    return jnp.minimum(u, 0.0) - jnp.log1p(jnp.exp(-jnp.abs(u)))


def _sigmoid(u):
    return 1.0 / (1.0 + jnp.exp(-u))


def _layer_norm(r, w, b):
    mu = jnp.mean(r, axis=-1, keepdims=True)
    c = r - mu
    var = jnp.mean(c * c, axis=-1, keepdims=True)
    return c * lax.rsqrt(var + LN_EPS) * w + b


def _rms_norm(o, w):
    return o * lax.rsqrt(jnp.mean(o * o, axis=-1, keepdims=True) + LN_EPS) * w


def _const_spec(shape):
    nd = len(shape)
    return pl.BlockSpec(shape, lambda *_: (0,) * nd, pipeline_mode=pl.Buffered(1))


def _params(semantics):
    return pltpu.CompilerParams(dimension_semantics=semantics, vmem_limit_bytes=VMEM_LIMIT_BYTES)


def _in_proj_kernel(x_ref, wab_ref, wr_ref, wt_ref, gw_ref, gb_ref, lb_ref,
                    qa_ref, ka_ref, va_ref, qg_ref, kg_ref, vg_ref, lg_ref, sgo_ref,
                    qh_ref, lf_ref, kh_ref, ih_ref, sho_ref, *, layer):
    lb_rows = [lb_ref[d:d + 1, :] for d in range(DEPTH)]
    mx = functools.reduce(jnp.maximum, lb_rows)
    es = [jnp.exp(r - mx) for r in lb_rows]
    tot = functools.reduce(jnp.add, es)
    soft = [e / tot for e in es]
    lb = functools.reduce(jnp.add, soft[:layer + 1]) - soft[0]
    lb_floored = jnp.maximum(lb, LB_FLOOR)
    one_minus_lb = 1.0 - lb
    w_hi, w_mid, _ = _split_bf16(gw_ref[...])

    for r0 in range(0, x_ref.shape[0], SUB_ROWS):
        rows = slice(r0, r0 + SUB_ROWS)
        xb = x_ref[rows, :].astype(BF16)

        def proj(w_ref, lo, width):
            return _dot(xb, w_ref[:, lo:lo + width])

        f = lb_floored + one_minus_lb * _sigmoid(proj(wt_ref, _T_CF, 512))
        lf_ref[rows, :] = jnp.log(f)
        kh_ref[rows, :] = (1.0 - f).astype(BF16)
        qa_ref[rows, :] = (proj(wab_ref, _A_Q, 512) * DIFF_QK_DIM ** -0.5).astype(BF16)
        ka_ref[rows, :] = proj(wab_ref, _A_K, 512).astype(BF16)
        gout = proj(wt_ref, _T_BG, 512)
        sgo_ref[rows, :] = (gout * _sigmoid(gout)).astype(BF16)
        va_ref[rows, :] = proj(wab_ref, _A_V, 512).astype(BF16)
        qh_ref[rows, :] = proj(wt_ref, _T_CQ, 512).astype(BF16)
        hout = proj(wt_ref, _T_CG, 512)
        sho_ref[rows, :] = (hout * _sigmoid(hout)).astype(BF16)
        ih_ref[rows, :] = proj(wt_ref, _T_CI, 512).astype(BF16)
        vg_ref[rows, :] = proj(wab_ref, _B_V, 512).astype(BF16)
        glr = proj(wr_ref, 0, LANES)
        a_hi, a_mid, _ = _split_bf16(glr)
        u = _dot(a_hi, w_hi) + (_dot(a_hi, w_mid) + _dot(a_mid, w_hi)) + gb_ref[...]
        lg_ref[rows, :] = _log_sigmoid(u) * (1.0 / GLA_GATE_NORM)
        qg_ref[rows, :] = (proj(wab_ref, _B_Q, 256) * GLA_DK ** -0.5).astype(BF16)
        kg_ref[rows, :] = proj(wab_ref, _B_K, 256).astype(BF16)


def _in_proj(x2d, wab, wr, wt, gw, gb, lb, layer):
    n = x2d.shape[0]
    tm = min(ROW_TILE, n)
    row = lambda width: pl.BlockSpec((tm, width), lambda i: (i, 0))
    out_widths = [(512, BF16), (512, BF16), (512, BF16),
                  (256, BF16), (256, BF16), (512, BF16), (256, F32), (512, BF16),
                  (512, BF16), (512, F32), (512, BF16), (512, BF16), (512, BF16)]
    return pl.pallas_call(
        functools.partial(_in_proj_kernel, layer=layer),
        grid=(n // tm,),
        in_specs=[row(D_MODEL), _const_spec(wab.shape), _const_spec(wr.shape),
                  _const_spec((D_MODEL, _T_MG)),
                  _const_spec(gw.shape), _const_spec(gb.shape), _const_spec(lb.shape)],
        out_specs=[row(w) for w, _ in out_widths],
        out_shape=[jax.ShapeDtypeStruct((n, w), dt) for w, dt in out_widths],
        compiler_params=_params(("parallel",)),
        name="in_proj",
    )(x2d, wab, wr, wt, gw, gb, lb)


def _diff_attn_kernel(q_ref, k_ref, v_ref, lam_ref, nw_ref, o_ref, m_scr, l_scr, acc_scr, *, layer):
    t = ATTN_TILE
    qi = pl.program_id(1)

    lane = lax.broadcasted_iota(jnp.int32, (t, LANES), 1)
    qs = []
    for h in range(DIFF_HEADS):
        q = q_ref[0, :, h * LANES:(h + 1) * LANES]
        zero = jnp.zeros_like(q)
        qs.append(jnp.concatenate([jnp.where(lane < DIFF_QK_DIM, q, zero),
                                   jnp.where(lane >= DIFF_QK_DIM, q, zero)], axis=0))

    m_scr[...] = jnp.full(m_scr.shape, -jnp.inf, F32)
    l_scr[...] = jnp.zeros(l_scr.shape, F32)
    acc_scr[...] = jnp.zeros(acc_scr.shape, F32)
    col = lax.broadcasted_iota(jnp.int32, (1, t), 1)
    ones = jnp.ones((t, LANES), BF16)

    def block(j, masked):
        off = pl.multiple_of(j * t, t)
        rel = ((j - qi) * t + col).astype(F32)
        if masked:
            r = lax.broadcasted_iota(jnp.int32, (t, t), 0)
            c = lax.broadcasted_iota(jnp.int32, (t, t), 1)
            keep = jnp.concatenate([r >= c, r >= c], axis=0)

        def scores(h):
            return _dot_nt(qs[h], k_ref[0, pl.ds(off, t), h * LANES:(h + 1) * LANES])

        s_next = scores(0)
        for h in range(DIFF_HEADS):
            s = s_next
            if h + 1 < DIFF_HEADS:
                s_next = scores(h + 1)
            s = s + 2.0 ** (-8.0 * (h + 1) / DIFF_HEADS) * rel
            if masked:
                s = jnp.where(keep, s, MASK_VALUE)
            m_prev = m_scr[h]
            m_next = jnp.maximum(m_prev, jnp.max(s, axis=-1, keepdims=True))
            corr = jnp.exp(m_prev - m_next)
            p = jnp.concatenate([jnp.exp(s[:, b0:b0 + LANES] - m_next) for b0 in range(0, t, LANES)],
                                axis=1).astype(BF16)
            vb = v_ref[0, pl.ds(off, t), h * LANES:(h + 1) * LANES]
            pv = _dot(p, jnp.concatenate([vb, ones], axis=1))
            acc_scr[h] = acc_scr[h] * corr + pv[:, :LANES]
            l_scr[h] = l_scr[h] * corr + pv[:, LANES:]
            m_scr[h] = m_next

    def body(j, carry):
        block(j, False)
        return carry

    lax.fori_loop(0, qi, body, 0)
    block(qi, True)

    lp = lam_ref[...]
    lam_init = 0.8 - 0.6 * math.exp(-0.3 * layer)
    lam = (jnp.exp(jnp.sum(lp[0:1] * lp[1:2], axis=-1, keepdims=True))
           - jnp.exp(jnp.sum(lp[2:3] * lp[3:4], axis=-1, keepdims=True)) + lam_init)
    nw = nw_ref[...]
    for h in range(DIFF_HEADS):
        o12 = acc_scr[h] / l_scr[h]
        o = o12[:t] - lam * o12[t:]
        o_ref[0, :, h * LANES:(h + 1) * LANES] = (_rms_norm(o, nw) * (1.0 - lam_init)).astype(o_ref.dtype)


def _diff_attn(qa, ka, va, lam_p, norm_w, layer):
    b, t, width = qa.shape
    tq = ATTN_TILE
    scr = pltpu.VMEM((DIFF_HEADS, 2 * tq, LANES), F32)
    return pl.pallas_call(
        functools.partial(_diff_attn_kernel, layer=layer),
        grid=(b, t // tq),
        in_specs=[pl.BlockSpec((1, tq, width), lambda bi, qi: (bi, qi, 0)),
                  pl.BlockSpec((1, t, width), lambda bi, qi: (bi, 0, 0)),
                  pl.BlockSpec((1, t, width), lambda bi, qi: (bi, 0, 0)),
                  _const_spec(lam_p.shape), _const_spec(norm_w.shape)],
        out_specs=pl.BlockSpec((1, tq, width), lambda bi, qi: (bi, qi, 0)),
        out_shape=jax.ShapeDtypeStruct((b, t, width), BF16),
        scratch_shapes=[scr, scr, scr],
        compiler_params=_params(("parallel", "arbitrary")),
        name="diff_attn",
    )(qa, ka, va, lam_p, norm_w)


_N_LEVELS = int(math.log2(CHUNK))
ROBUST_UNIT = SUBLANES


class _UnitLayout:
    def __init__(self, unit):
        self.unit = unit
        self.level0 = int(math.log2(unit))
        self.n_units = CHUNK // unit
        self.row_p = max(_N_LEVELS - self.level0 - 1, 0) * self.n_units
        self.row_q = self.row_p + self.n_units
        self.row_last = self.row_q + self.n_units
        self.packed_rows = -(-(self.row_last + 1) // 16) * 16

    def factor_row(self, li):
        return (li - self.level0 - 1) * self.n_units


def _level_table(lay, fast):
    i = np.arange(CHUNK)[:, None]
    j = np.arange(CHUNK)[None, :]
    lvl = np.where(i > j, np.floor(np.log2(np.maximum(np.bitwise_xor(i, j), 1))), -1.0)
    if fast:
        lvl = np.where(i >= j, np.maximum(lvl, lay.level0 - 1), -1.0)
    return lvl.astype(np.int32)


def _gate_sum_matrix(lay):
    c, nu, u = CHUNK, lay.n_units, lay.unit
    m = np.zeros((c + lay.packed_rows, c), np.float32)
    m[:c] = np.tril(np.ones((c, c), np.float32))
    for li in range(lay.level0 + 1, _N_LEVELS):
        half = (1 << li) // u
        for unit in range(nu):
            first = (unit // (2 * half)) * 2 * half
            boundary = u * (first + half)
            if unit - first >= half:
                m[c + lay.factor_row(li) + unit, boundary:u * unit] = 1.0
            else:
                m[c + lay.factor_row(li) + unit, u * unit + u:boundary] = 1.0
    for unit in range(nu):
        m[c + lay.row_p + unit, :u * unit] = 1.0
        m[c + lay.row_q + unit, u * unit + u:] = 1.0
    m[c + lay.row_last, :] = 1.0
    return np.concatenate([m, m], axis=1)


def _unit_rows(ref, blk, first_row, stride, lay):
    parts = [jnp.broadcast_to(ref[blk, pl.ds(first_row + stride * unit, 1), :], (lay.unit, LANES))
             for unit in range(lay.n_units)]
    return parts[0] if len(parts) == 1 else jnp.concatenate(parts, axis=0)


def _gla_kernel(q_ref, k_ref, v_ref, g_ref, og_ref, nw_ref, gsum_r_ref, lvl_r_ref, gsum_f_ref, lvl_f_ref,
                o_ref, st_scr, cum_scr, fac_scr, *, dk, n_heads, fast_unit):
    heads_per_block = LANES // dk
    n_blocks = n_heads // heads_per_block
    c = CHUNK
    sl = SUBLANES
    pad = SUBLANES

    @pl.when(pl.program_id(1) == 0)
    def _():
        st_scr[...] = jnp.zeros(st_scr.shape, F32)

    cum_scr[:, 0:pad, :] = jnp.zeros((SCRATCH_SLOTS * n_blocks, pad, LANES), F32)
    nw = nw_ref[...]
    lane = lax.broadcasted_iota(jnp.int32, (c, LANES), 1)
    sub = lax.broadcasted_iota(jnp.int32, (sl, LANES), 0)

    def sweep(fast):
        lay = _UnitLayout(fast_unit if fast else ROBUST_UNIT)
        gsum = (gsum_f_ref if fast else gsum_r_ref)[...]
        lvl = (lvl_f_ref if fast else lvl_r_ref)[...]

        def gate_sums(rows, slot):
            loaded = []
            for blk in range(n_blocks):
                lanes = slice(blk * LANES, (blk + 1) * LANES)
                g = g_ref[0, rows, lanes]
                kb = k_ref[0, rows, lanes]
                qf = q_ref[0, rows, lanes].astype(F32)
                g_hi, g_mid, _ = _split_bf16(g)
                sums = _dot(gsum, jnp.concatenate([g_hi, g_mid], axis=0))
                cum_scr[slot * n_blocks + blk, pad:pad + c, :] = sums[:c]
                fac_scr[slot * n_blocks + blk, 0:lay.packed_rows, :] = jnp.exp(sums[c:])
                loaded.append((g, kb, qf, sums[:c]))
            return loaded

        def operands(loaded, slot):
            out = []
            for lane_blk, (g, kb, qf, cum) in enumerate(loaded):
                blk = slot * n_blocks + lane_blk
                kf = kb.astype(F32)
                before = cum - _unit_rows(cum_scr, blk, pad - 1, lay.unit, lay)
                q_u = qf * jnp.exp(before)
                k_u = kf * jnp.exp(_unit_rows(cum_scr, blk, pad + lay.unit - 1, lay.unit, lay) - cum)
                terms = []
                if fast:
                    terms.append((lay.level0 - 1, q_u.astype(BF16), (kf * jnp.exp(-before)).astype(BF16)))
                else:
                    terms.append((0, (qf * jnp.exp(g)).astype(BF16), kb))
                    m2 = jnp.concatenate(
                        [jnp.where(sub < 4,
                                   jnp.broadcast_to(cum_scr[blk, pl.ds(pad + sl * t + 1, 1), :], (sl, LANES)),
                                   jnp.broadcast_to(cum_scr[blk, pl.ds(pad + sl * t + 5, 1), :], (sl, LANES)))
                         for t in range(lay.n_units)], axis=0)
                    for li, m in ((1, m2), (2, _unit_rows(cum_scr, blk, pad + 3, sl, lay))):
                        e = jnp.exp(-jnp.abs(cum - m))
                        terms.append((li, (qf * e).astype(BF16), (kf * e).astype(BF16)))
                if lay.level0 < _N_LEVELS:
                    terms.append((lay.level0, q_u.astype(BF16), k_u.astype(BF16)))
                for li in range(lay.level0 + 1, _N_LEVELS):
                    f = _unit_rows(fac_scr, blk, lay.factor_row(li), 1, lay)
                    terms.append((li, (q_u * f).astype(BF16), (k_u * f).astype(BF16)))
                q_in = (q_u * _unit_rows(fac_scr, blk, lay.row_p, 1, lay)).astype(BF16)
                k_out = (k_u * _unit_rows(fac_scr, blk, lay.row_q, 1, lay)).astype(BF16)
                decay = fac_scr[blk, pl.ds(lay.row_last, 1), :]
                qk = None if fast else qf * kf
                out.append((terms, q_in, k_out, decay, qk))
            return out

        def score_parts(ops):
            heads = []
            for blk, (terms, q_in, k_out, decay, qk) in enumerate(ops):
                for hb in range(heads_per_block):
                    in_head = (lane >= hb * dk) & (lane < (hb + 1) * dk)
                    sel = (lambda a: a) if heads_per_block == 1 else (
                        lambda a, in_head=in_head: jnp.where(in_head, a, jnp.zeros_like(a)))
                    parts = [(li, _dot_nt(sel(q_l), k_l)) for li, q_l, k_l in terms]
                    heads.append((blk * heads_per_block + hb, sel, parts, q_in, k_out, decay, qk))
            return heads

        def outputs(rows, heads):
            outs = []
            for h, sel, parts, q_in, k_out, decay, qk in heads:
                vh = v_ref[0, rows, h * LANES:(h + 1) * LANES]
                scores = jnp.zeros((c, c), F32)
                for li, part in parts:
                    scores = jnp.where(lvl == li, part, scores)
                st = st_scr[h]
                o = _dot(scores.astype(BF16), vh) + _dot_nt(sel(q_in), st.astype(BF16))
                if not fast:
                    o = o + jnp.sum(sel(qk), axis=-1, keepdims=True) * vh.astype(F32)
                outs.append((h, vh, st, o, k_out, decay))

            for h, vh, st, o, k_out, decay in outs:
                hl = slice(h * LANES, (h + 1) * LANES)
                st_scr[h] = st * decay + _dot_tn(vh, k_out)
                og = og_ref[0, rows, hl].astype(F32)
                o_ref[0, rows, hl] = (_rms_norm(o, nw) * og).astype(o_ref.dtype)

        n_chunks = REC_ROWS // c
        if fast:
            rows_of = lambda ci: slice(ci * c, (ci + 1) * c)
            loaded = {ci: gate_sums(rows_of(ci), ci % SCRATCH_SLOTS) for ci in range(min(2, n_chunks))}
            ops = {0: operands(loaded.pop(0), 0)}
            for ci in range(n_chunks):
                heads = score_parts(ops.pop(ci))
                if ci + 2 < n_chunks:
                    loaded[ci + 2] = gate_sums(rows_of(ci + 2), (ci + 2) % SCRATCH_SLOTS)
                if ci + 1 < n_chunks:
                    ops[ci + 1] = operands(loaded.pop(ci + 1), (ci + 1) % SCRATCH_SLOTS)
                outputs(rows_of(ci), heads)
        else:
            def pair(p, carry):
                for half in range(2):
                    rows = pl.ds(pl.multiple_of((2 * p + half) * c, c), c)
                    outputs(rows, score_parts(operands(gate_sums(rows, half), half)))
                return carry

            lax.fori_loop(0, n_chunks // 2, pair, 0)

    worst = None
    for r0 in range(0, REC_ROWS, fast_unit):
        tot = jnp.sum(g_ref[0, r0:r0 + fast_unit, :], axis=0, keepdims=True)
        worst = tot if worst is None else jnp.minimum(worst, tot)
    mild = jnp.min(worst) >= -FAST_PATH_MAX_DECAY

    @pl.when(mild)
    def _():
        sweep(True)

    @pl.when(jnp.logical_not(mild))
    def _():
        sweep(False)


def _gated_linear_attention(q, k, v, g, og, norm_w, dk, n_heads, fast_unit, name):
    b, t, hk = q.shape
    hv = v.shape[-1]
    rows = REC_ROWS
    n_blocks = hk // LANES
    lay_r, lay_f = _UnitLayout(ROBUST_UNIT), _UnitLayout(fast_unit)
    consts = [norm_w,
              jnp.asarray(_gate_sum_matrix(lay_r), BF16), jnp.asarray(_level_table(lay_r, False)),
              jnp.asarray(_gate_sum_matrix(lay_f), BF16), jnp.asarray(_level_table(lay_f, True))]
    seq = lambda width: pl.BlockSpec((1, rows, width), lambda bi, ci: (bi, ci, 0))
    return pl.pallas_call(
        functools.partial(_gla_kernel, dk=dk, n_heads=n_heads, fast_unit=fast_unit),
        grid=(b, t // rows),
        in_specs=[seq(hk), seq(hk), seq(hv), seq(hk), seq(hv)] + [_const_spec(a.shape) for a in consts],
        out_specs=seq(hv),
        out_shape=jax.ShapeDtypeStruct((b, t, hv), BF16),
        scratch_shapes=[pltpu.VMEM((n_heads, LANES, LANES), F32),
                        pltpu.VMEM((SCRATCH_SLOTS * n_blocks, SUBLANES + CHUNK, LANES), F32),
                        pltpu.VMEM((SCRATCH_SLOTS * n_blocks, max(lay_r.packed_rows, lay_f.packed_rows), LANES), F32)],
        compiler_params=_params(("parallel", "arbitrary")),
        name=name,
    )(q, k, v, g, og, *consts)


def _merge_kernel(x_ref, oa_ref, ob_ref, oc_ref, wt_ref, wbr_ref, wout_ref, lnw_ref, lnb_ref, o_ref):
    for r0 in range(0, x_ref.shape[0], SUB_ROWS):
        rows = slice(r0, r0 + SUB_ROWS)
        x = x_ref[rows, :]
        xb = x.astype(BF16)
        mixed = None
        for n, br_ref in enumerate((oa_ref, ob_ref, oc_ref)):
            gate = _sigmoid(_dot(xb, wt_ref[:, _T_MG + n * D_MODEL:_T_MG + (n + 1) * D_MODEL]))
            y = gate * _dot(br_ref[rows, :], wbr_ref[n])
            mixed = y if mixed is None else mixed + y
        mix = _dot(mixed.astype(BF16), wout_ref[...])
        o_ref[rows, :] = _layer_norm(ALPHA * x + mix, lnw_ref[...], lnb_ref[...])


def _merge(x2d, oa, ob, oc, wt, wbr, wout, lnw, lnb):
    n = x2d.shape[0]
    tm = min(ROW_TILE, n)
    row = lambda width: pl.BlockSpec((tm, width), lambda i: (i, 0))
    return pl.pallas_call(
        _merge_kernel,
        grid=(n // tm,),
        in_specs=[row(D_MODEL), row(BRANCH_WIDTH), row(BRANCH_WIDTH), row(BRANCH_WIDTH),
                  _const_spec(wt.shape), _const_spec(wbr.shape), _const_spec(wout.shape),
                  _const_spec(lnw.shape), _const_spec(lnb.shape)],
        out_specs=row(D_MODEL),
        out_shape=jax.ShapeDtypeStruct((n, D_MODEL), F32),
        compiler_params=_params(("parallel",)),
        name="merge",
    )(x2d, oa, ob, oc, wt, wbr, wout, lnw, lnb)


def _mlp_kernel(x_ref, wup_ref, wdn_ref, lnw_ref, lnb_ref, o_ref):
    for r0 in range(0, x_ref.shape[0], SUB_ROWS):
        rows = slice(r0, r0 + SUB_ROWS)
        x = x_ref[rows, :]
        xb = x.astype(BF16)
        acc = None
        for f0 in range(0, D_FF, D_MODEL):
            hmid = jnp.maximum(_dot(xb, wup_ref[:, f0:f0 + D_MODEL]), 0.0)
            part = _dot((hmid * hmid).astype(BF16), wdn_ref[f0:f0 + D_MODEL, :])
            acc = part if acc is None else acc + part
        o_ref[rows, :] = _layer_norm(ALPHA * x + acc, lnw_ref[...], lnb_ref[...])


def _mlp(x2d, wup, wdn, lnw, lnb):
    n = x2d.shape[0]
    tm = min(ROW_TILE, n)
    row = pl.BlockSpec((tm, D_MODEL), lambda i: (i, 0))
    return pl.pallas_call(
        _mlp_kernel,
        grid=(n // tm,),
        in_specs=[row, _const_spec(wup.shape), _const_spec(wdn.shape),
                  _const_spec(lnw.shape), _const_spec(lnb.shape)],
        out_specs=row,
        out_shape=jax.ShapeDtypeStruct((n, D_MODEL), F32),
        compiler_params=_params(("parallel",)),
        name="mlp",
    )(x2d, wup, wdn, lnw, lnb)


def _layer(x2d, b, t, layer, w_in, gla_w_gate, gla_b_gate, diff_lambda, diff_norm_w, gla_norm_w,
           hgrn_norm_w, hgrn_lb, w_branch, w_out, ln1_w, ln1_b, w_up, w_down, ln2_w, ln2_b):
    n = b * t
    w = w_in[layer]
    wab = w[:, :_B_R].astype(BF16)
    wr = jnp.pad(w[:, _B_R:_B_G], ((0, 0), (0, LANES - GLA_GATE_RANK))).astype(BF16)
    wt = w[:, _B_G:].astype(BF16)
    gw = jnp.pad(gla_w_gate[layer], ((0, LANES - GLA_GATE_RANK), (0, 0)))
    gb = gla_b_gate[layer][None, :]

    (qa, ka, va, qg, kg, vg, lg, sgo, qh, lf, kh, ih, sho) = _in_proj(
        x2d, wab, wr, wt, gw, gb, hgrn_lb, layer)
    seq = lambda a: a.reshape(b, t, a.shape[-1])

    oa = _diff_attn(seq(qa), seq(ka), seq(va), diff_lambda[layer], diff_norm_w[layer][None, :], layer)
    ob = _gated_linear_attention(seq(qg), seq(kg), seq(vg), seq(lg), seq(sgo),
                                 gla_norm_w[layer][None, :], GLA_DK, GLA_HEADS, GLA_FAST_UNIT, "gla")
    oc = _gated_linear_attention(seq(qh), seq(kh), seq(ih), seq(lf), seq(sho),
                                 hgrn_norm_w[layer][None, :], HGRN_EXPAND, HGRN_HEADS, HGRN_FAST_UNIT, "hgrn")

    flat = lambda a: a.reshape(n, a.shape[-1])
    x1 = _merge(x2d, flat(oa), flat(ob), flat(oc), wt, w_branch[layer], w_out[layer],
                ln1_w[layer][None, :], ln1_b[layer][None, :])
    return _mlp(x1, w_up[layer], w_down[layer], ln2_w[layer][None, :], ln2_b[layer][None, :])


def kernel(x, w_in, gla_w_gate, gla_b_gate, diff_lambda, diff_norm_w, gla_norm_w, hgrn_norm_w, hgrn_lb,
           w_branch, w_out, ln1_w, ln1_b, w_up, w_down, ln2_w, ln2_b):
    b, t, d = x.shape
    assert d == D_MODEL and w_in.shape == (DEPTH, D_MODEL, IN_WIDTH)
    assert t % REC_ROWS == 0 and t % ATTN_TILE == 0 and (b * t) % ROW_TILE == 0 and ROW_TILE % SUB_ROWS == 0
    x2d = x.reshape(b * t, d)
    w_branch, w_out, w_up, w_down = (a.astype(BF16) for a in (w_branch, w_out, w_up, w_down))
    for layer in range(DEPTH):
        x2d = _layer(x2d, b, t, layer, w_in, gla_w_gate, gla_b_gate, diff_lambda, diff_norm_w,
                     gla_norm_w, hgrn_norm_w, hgrn_lb, w_branch, w_out, ln1_w, ln1_b,
                     w_up, w_down, ln2_w, ln2_b)
    return x2d.reshape(b, t, d)
```

```python
import functools
import math

import numpy as np
import jax
import jax.numpy as jnp
from jax import lax
from jax.experimental import pallas as pl
from jax.experimental.pallas import tpu as pltpu

D_MODEL = 1024
DEPTH = 2
DIFF_HEADS = 4
DIFF_QK_DIM = 64
DIFF_V_DIM = 128
GLA_HEADS = 4
GLA_DK = 64
GLA_DV = 128
GLA_GATE_RANK = 16
GLA_GATE_NORM = 16.0
HGRN_HEADS = 4
HGRN_EXPAND = 128
HGRN_DV = 128
N_BRANCHES = 3
BRANCH_WIDTH = 512
D_FF = 4 * D_MODEL
ALPHA = (2 * DEPTH) ** 0.25
LN_EPS = 1e-5
MASK_VALUE = -1e30
LB_FLOOR = 1e-30

LANES = 128
SUBLANES = 8
VMEM_LIMIT_BYTES = 56 * 1024 * 1024

ROW_TILE = 1024
SUB_ROWS = 256
ATTN_TILE = 512
CHUNK = 128
REC_ROWS = 2048
FAST_PATH_MAX_DECAY = 60.0
SCRATCH_SLOTS = 3
GLA_FAST_UNIT = 128
HGRN_FAST_UNIT = 32

F32 = jnp.float32
BF16 = jnp.bfloat16

_A_Q, _A_K, _A_V = 0, 512, 1024
_B_Q, _B_K, _B_V, _B_R, _B_G = 1536, 1792, 2048, 2560, 2576
_C_Q, _C_F, _C_I, _C_G = 3088, 3600, 4112, 4624
_MG = 5136
IN_WIDTH = _MG + N_BRANCHES * D_MODEL
_T_BG, _T_CQ, _T_CF, _T_CI, _T_CG, _T_MG = (o - _B_G for o in (_B_G, _C_Q, _C_F, _C_I, _C_G, _MG))


def _dot(a, b):
    return jnp.dot(a, b, preferred_element_type=F32)


def _dot_nt(a, b):
    return lax.dot_general(a, b, (((1,), (1,)), ((), ())), preferred_element_type=F32)


def _dot_tn(a, b):
    return lax.dot_general(a, b, (((0,), (0,)), ((), ())), preferred_element_type=F32)


def _split_bf16(a):
    hi = a.astype(BF16)
    r = a - hi.astype(F32)
    mid = r.astype(BF16)
    lo = (r - mid.astype(F32)).astype(BF16)
    return hi, mid, lo


def _log_sigmoid(u):
    return jnp.minimum(u, 0.0) - jnp.log1p(jnp.exp(-jnp.abs(u)))


def _sigmoid(u):
    return 1.0 / (1.0 + jnp.exp(-u))


def _layer_norm(r, w, b):
    mu = jnp.mean(r, axis=-1, keepdims=True)
    c = r - mu
    var = jnp.mean(c * c, axis=-1, keepdims=True)
    return c * lax.rsqrt(var + LN_EPS) * w + b


def _rms_norm(o, w):
    return o * lax.rsqrt(jnp.mean(o * o, axis=-1, keepdims=True) + LN_EPS) * w


def _const_spec(shape):
    nd = len(shape)
    return pl.BlockSpec(shape, lambda *_: (0,) * nd, pipeline_mode=pl.Buffered(1))


def _params(semantics):
    return pltpu.CompilerParams(dimension_semantics=semantics, vmem_limit_bytes=VMEM_LIMIT_BYTES)


def _in_proj_kernel(x_ref, wab_ref, wr_ref, wt_ref, gw_ref, gb_ref, lb_ref,
                    qa_ref, ka_ref, va_ref, qg_ref, kg_ref, vg_ref, lg_ref, sgo_ref,
                    qh_ref, lf_ref, kh_ref, ih_ref, sho_ref, *, layer):
    lb_rows = [lb_ref[d:d + 1, :] for d in range(DEPTH)]
    mx = functools.reduce(jnp.maximum, lb_rows)
    es = [jnp.exp(r - mx) for r in lb_rows]
    tot = functools.reduce(jnp.add, es)
    soft = [e / tot for e in es]
    lb = functools.reduce(jnp.add, soft[:layer + 1]) - soft[0]
    lb_floored = jnp.maximum(lb, LB_FLOOR)
    one_minus_lb = 1.0 - lb
    w_hi, w_mid, _ = _split_bf16(gw_ref[...])

    for r0 in range(0, x_ref.shape[0], SUB_ROWS):
        rows = slice(r0, r0 + SUB_ROWS)
        xb = x_ref[rows, :].astype(BF16)

        def proj(w_ref, lo, width):
            return _dot(xb, w_ref[:, lo:lo + width])

        f = lb_floored + one_minus_lb * _sigmoid(proj(wt_ref, _T_CF, 512))
        lf_ref[rows, :] = jnp.log(f)
        kh_ref[rows, :] = (1.0 - f).astype(BF16)
        qa_ref[rows, :] = (proj(wab_ref, _A_Q, 512) * DIFF_QK_DIM ** -0.5).astype(BF16)
        ka_ref[rows, :] = proj(wab_ref, _A_K, 512).astype(BF16)
        gout = proj(wt_ref, _T_BG, 512)
        sgo_ref[rows, :] = (gout * _sigmoid(gout)).astype(BF16)
        va_ref[rows, :] = proj(wab_ref, _A_V, 512).astype(BF16)
        qh_ref[rows, :] = proj(wt_ref, _T_CQ, 512).astype(BF16)
        hout = proj(wt_ref, _T_CG, 512)
        sho_ref[rows, :] = (hout * _sigmoid(hout)).astype(BF16)
        ih_ref[rows, :] = proj(wt_ref, _T_CI, 512).astype(BF16)
        vg_ref[rows, :] = proj(wab_ref, _B_V, 512).astype(BF16)
        glr = proj(wr_ref, 0, LANES)
        a_hi, a_mid, _ = _split_bf16(glr)
        u = _dot(a_hi, w_hi) + (_dot(a_hi, w_mid) + _dot(a_mid, w_hi)) + gb_ref[...]
        lg_ref[rows, :] = _log_sigmoid(u) * (1.0 / GLA_GATE_NORM)
        qg_ref[rows, :] = (proj(wab_ref, _B_Q, 256) * GLA_DK ** -0.5).astype(BF16)
        kg_ref[rows, :] = proj(wab_ref, _B_K, 256).astype(BF16)


def _in_proj(x2d, wab, wr, wt, gw, gb, lb, layer):
    n = x2d.shape[0]
    tm = min(ROW_TILE, n)
    row = lambda width: pl.BlockSpec((tm, width), lambda i: (i, 0))
    out_widths = [(512, BF16), (512, BF16), (512, BF16),
                  (256, BF16), (256, BF16), (512, BF16), (256, F32), (512, BF16),
                  (512, BF16), (512, F32), (512, BF16), (512, BF16), (512, BF16)]
    return pl.pallas_call(
        functools.partial(_in_proj_kernel, layer=layer),
        grid=(n // tm,),
        in_specs=[row(D_MODEL), _const_spec(wab.shape), _const_spec(wr.shape),
                  _const_spec((D_MODEL, _T_MG)),
                  _const_spec(gw.shape), _const_spec(gb.shape), _const_spec(lb.shape)],
        out_specs=[row(w) for w, _ in out_widths],
        out_shape=[jax.ShapeDtypeStruct((n, w), dt) for w, dt in out_widths],
        compiler_params=_params(("parallel",)),
        name="in_proj",
    )(x2d, wab, wr, wt, gw, gb, lb)


def _diff_attn_kernel(q_ref, k_ref, v_ref, lam_ref, nw_ref, o_ref, m_scr, l_scr, acc_scr, *, layer):
    t = ATTN_TILE
    qi = pl.program_id(1)

    lane = lax.broadcasted_iota(jnp.int32, (t, LANES), 1)
    qs = []
    for h in range(DIFF_HEADS):
        q = q_ref[0, :, h * LANES:(h + 1) * LANES]
        zero = jnp.zeros_like(q)
        qs.append(jnp.concatenate([jnp.where(lane < DIFF_QK_DIM, q, zero),
                                   jnp.where(lane >= DIFF_QK_DIM, q, zero)], axis=0))

    m_scr[...] = jnp.full(m_scr.shape, -jnp.inf, F32)
    l_scr[...] = jnp.zeros(l_scr.shape, F32)
    acc_scr[...] = jnp.zeros(acc_scr.shape, F32)
    col = lax.broadcasted_iota(jnp.int32, (1, t), 1)
    ones = jnp.ones((t, LANES), BF16)

    def block(j, masked):
        off = pl.multiple_of(j * t, t)
        rel = ((j - qi) * t + col).astype(F32)
        if masked:
            r = lax.broadcasted_iota(jnp.int32, (t, t), 0)
            c = lax.broadcasted_iota(jnp.int32, (t, t), 1)
            keep = jnp.concatenate([r >= c, r >= c], axis=0)

        def scores(h):
            return _dot_nt(qs[h], k_ref[0, pl.ds(off, t), h * LANES:(h + 1) * LANES])

        s_next = scores(0)
        for h in range(DIFF_HEADS):
            s = s_next
            if h + 1 < DIFF_HEADS:
                s_next = scores(h + 1)
            s = s + 2.0 ** (-8.0 * (h + 1) / DIFF_HEADS) * rel
            if masked:
                s = jnp.where(keep, s, MASK_VALUE)
            m_prev = m_scr[h]
            m_next = jnp.maximum(m_prev, jnp.max(s, axis=-1, keepdims=True))
            corr = jnp.exp(m_prev - m_next)
            p = jnp.concatenate([jnp.exp(s[:, b0:b0 + LANES] - m_next) for b0 in range(0, t, LANES)],
                                axis=1).astype(BF16)
            vb = v_ref[0, pl.ds(off, t), h * LANES:(h + 1) * LANES]
            pv = _dot(p, jnp.concatenate([vb, ones], axis=1))
            acc_scr[h] = acc_scr[h] * corr + pv[:, :LANES]
            l_scr[h] = l_scr[h] * corr + pv[:, LANES:]
            m_scr[h] = m_next

    def body(j, carry):
        block(j, False)
        return carry

    lax.fori_loop(0, qi, body, 0)
    block(qi, True)

    lp = lam_ref[...]
    lam_init = 0.8 - 0.6 * math.exp(-0.3 * layer)
    lam = (jnp.exp(jnp.sum(lp[0:1] * lp[1:2], axis=-1, keepdims=True))
           - jnp.exp(jnp.sum(lp[2:3] * lp[3:4], axis=-1, keepdims=True)) + lam_init)
    nw = nw_ref[...]
    for h in range(DIFF_HEADS):
        o12 = acc_scr[h] / l_scr[h]
        o = o12[:t] - lam * o12[t:]
        o_ref[0, :, h * LANES:(h + 1) * LANES] = (_rms_norm(o, nw) * (1.0 - lam_init)).astype(o_ref.dtype)


def _diff_attn(qa, ka, va, lam_p, norm_w, layer):
    b, t, width = qa.shape
    tq = ATTN_TILE
    scr = pltpu.VMEM((DIFF_HEADS, 2 * tq, LANES), F32)
    return pl.pallas_call(
        functools.partial(_diff_attn_kernel, layer=layer),
        grid=(b, t // tq),
        in_specs=[pl.BlockSpec((1, tq, width), lambda bi, qi: (bi, qi, 0)),
                  pl.BlockSpec((1, t, width), lambda bi, qi: (bi, 0, 0)),
                  pl.BlockSpec((1, t, width), lambda bi, qi: (bi, 0, 0)),
                  _const_spec(lam_p.shape), _const_spec(norm_w.shape)],
        out_specs=pl.BlockSpec((1, tq, width), lambda bi, qi: (bi, qi, 0)),
        out_shape=jax.ShapeDtypeStruct((b, t, width), BF16),
        scratch_shapes=[scr, scr, scr],
        compiler_params=_params(("parallel", "arbitrary")),
        name="diff_attn",
    )(qa, ka, va, lam_p, norm_w)


_N_LEVELS = int(math.log2(CHUNK))
ROBUST_UNIT = SUBLANES


class _UnitLayout:
    def __init__(self, unit):
        self.unit = unit
        self.level0 = int(math.log2(unit))
        self.n_units = CHUNK // unit
        self.row_p = max(_N_LEVELS - self.level0 - 1, 0) * self.n_units
        self.row_q = self.row_p + self.n_units
        self.row_last = self.row_q + self.n_units
        self.packed_rows = -(-(self.row_last + 1) // 16) * 16

    def factor_row(self, li):
        return (li - self.level0 - 1) * self.n_units


def _level_table(lay, fast):
    i = np.arange(CHUNK)[:, None]
    j = np.arange(CHUNK)[None, :]
    lvl = np.where(i > j, np.floor(np.log2(np.maximum(np.bitwise_xor(i, j), 1))), -1.0)
    if fast:
        lvl = np.where(i >= j, np.maximum(lvl, lay.level0 - 1), -1.0)
    return lvl.astype(np.int32)


def _gate_sum_matrix(lay):
    c, nu, u = CHUNK, lay.n_units, lay.unit
    m = np.zeros((c + lay.packed_rows, c), np.float32)
    m[:c] = np.tril(np.ones((c, c), np.float32))
    for li in range(lay.level0 + 1, _N_LEVELS):
        half = (1 << li) // u
        for unit in range(nu):
            first = (unit // (2 * half)) * 2 * half
            boundary = u * (first + half)
            if unit - first >= half:
                m[c + lay.factor_row(li) + unit, boundary:u * unit] = 1.0
            else:
                m[c + lay.factor_row(li) + unit, u * unit + u:boundary] = 1.0
    for unit in range(nu):
        m[c + lay.row_p + unit, :u * unit] = 1.0
        m[c + lay.row_q + unit, u * unit + u:] = 1.0
    m[c + lay.row_last, :] = 1.0
    return np.concatenate([m, m], axis=1)


def _unit_rows(ref, blk, first_row, stride, lay):
    parts = [jnp.broadcast_to(ref[blk, pl.ds(first_row + stride * unit, 1), :], (lay.unit, LANES))
             for unit in range(lay.n_units)]
    return parts[0] if len(parts) == 1 else jnp.concatenate(parts, axis=0)


def _gla_kernel(q_ref, k_ref, v_ref, g_ref, og_ref, nw_ref, gsum_r_ref, lvl_r_ref, gsum_f_ref, lvl_f_ref,
                o_ref, st_scr, cum_scr, fac_scr, *, dk, n_heads, fast_unit):
    heads_per_block = LANES // dk
    n_blocks = n_heads // heads_per_block
    c = CHUNK
    sl = SUBLANES
    pad = SUBLANES

    @pl.when(pl.program_id(1) == 0)
    def _():
        st_scr[...] = jnp.zeros(st_scr.shape, F32)

    cum_scr[:, 0:pad, :] = jnp.zeros((SCRATCH_SLOTS * n_blocks, pad, LANES), F32)
    nw = nw_ref[...]
    lane = lax.broadcasted_iota(jnp.int32, (c, LANES), 1)
    sub = lax.broadcasted_iota(jnp.int32, (sl, LANES), 0)

    def sweep(fast):
        lay = _UnitLayout(fast_unit if fast else ROBUST_UNIT)
        gsum = (gsum_f_ref if fast else gsum_r_ref)[...]
        lvl = (lvl_f_ref if fast else lvl_r_ref)[...]

        def gate_sums(rows, slot):
            loaded = []
            for blk in range(n_blocks):
                lanes = slice(blk * LANES, (blk + 1) * LANES)
                g = g_ref[0, rows, lanes]
                kb = k_ref[0, rows, lanes]
                qf = q_ref[0, rows, lanes].astype(F32)
                g_hi, g_mid, _ = _split_bf16(g)
                sums = _dot(gsum, jnp.concatenate([g_hi, g_mid], axis=0))
                cum_scr[slot * n_blocks + blk, pad:pad + c, :] = sums[:c]
                fac_scr[slot * n_blocks + blk, 0:lay.packed_rows, :] = jnp.exp(sums[c:])
                loaded.append((g, kb, qf, sums[:c]))
            return loaded

        def operands(loaded, slot):
            out = []
            for lane_blk, (g, kb, qf, cum) in enumerate(loaded):
                blk = slot * n_blocks + lane_blk
                kf = kb.astype(F32)
                before = cum - _unit_rows(cum_scr, blk, pad - 1, lay.unit, lay)
                q_u = qf * jnp.exp(before)
                k_u = kf * jnp.exp(_unit_rows(cum_scr, blk, pad + lay.unit - 1, lay.unit, lay) - cum)
                terms = []
                if fast:
                    terms.append((lay.level0 - 1, q_u.astype(BF16), (kf * jnp.exp(-before)).astype(BF16)))
                else:
                    terms.append((0, (qf * jnp.exp(g)).astype(BF16), kb))
                    m2 = jnp.concatenate(
                        [jnp.where(sub < 4,
                                   jnp.broadcast_to(cum_scr[blk, pl.ds(pad + sl * t + 1, 1), :], (sl, LANES)),
                                   jnp.broadcast_to(cum_scr[blk, pl.ds(pad + sl * t + 5, 1), :], (sl, LANES)))
                         for t in range(lay.n_units)], axis=0)
                    for li, m in ((1, m2), (2, _unit_rows(cum_scr, blk, pad + 3, sl, lay))):
                        e = jnp.exp(-jnp.abs(cum - m))
                        terms.append((li, (qf * e).astype(BF16), (kf * e).astype(BF16)))
                if lay.level0 < _N_LEVELS:
                    terms.append((lay.level0, q_u.astype(BF16), k_u.astype(BF16)))
                for li in range(lay.level0 + 1, _N_LEVELS):
                    f = _unit_rows(fac_scr, blk, lay.factor_row(li), 1, lay)
                    terms.append((li, (q_u * f).astype(BF16), (k_u * f).astype(BF16)))
                q_in = (q_u * _unit_rows(fac_scr, blk, lay.row_p, 1, lay)).astype(BF16)
                k_out = (k_u * _unit_rows(fac_scr, blk, lay.row_q, 1, lay)).astype(BF16)
                decay = fac_scr[blk, pl.ds(lay.row_last, 1), :]
                qk = None if fast else qf * kf
                out.append((terms, q_in, k_out, decay, qk))
            return out

        def score_parts(ops):
            heads = []
            for blk, (terms, q_in, k_out, decay, qk) in enumerate(ops):
                for hb in range(heads_per_block):
                    in_head = (lane >= hb * dk) & (lane < (hb + 1) * dk)
                    sel = (lambda a: a) if heads_per_block == 1 else (
                        lambda a, in_head=in_head: jnp.where(in_head, a, jnp.zeros_like(a)))
                    parts = [(li, _dot_nt(sel(q_l), k_l)) for li, q_l, k_l in terms]
                    heads.append((blk * heads_per_block + hb, sel, parts, q_in, k_out, decay, qk))
            return heads

        def outputs(rows, heads):
            outs = []
            for h, sel, parts, q_in, k_out, decay, qk in heads:
                vh = v_ref[0, rows, h * LANES:(h + 1) * LANES]
                scores = jnp.zeros((c, c), F32)
                for li, part in parts:
                    scores = jnp.where(lvl == li, part, scores)
                st = st_scr[h]
                o = _dot(scores.astype(BF16), vh) + _dot_nt(sel(q_in), st.astype(BF16))
                if not fast:
                    o = o + jnp.sum(sel(qk), axis=-1, keepdims=True) * vh.astype(F32)
                outs.append((h, vh, st, o, k_out, decay))

            for h, vh, st, o, k_out, decay in outs:
                hl = slice(h * LANES, (h + 1) * LANES)
                st_scr[h] = st * decay + _dot_tn(vh, k_out)
                og = og_ref[0, rows, hl].astype(F32)
                o_ref[0, rows, hl] = (_rms_norm(o, nw) * og).astype(o_ref.dtype)

        n_chunks = REC_ROWS // c
        if fast:
            rows_of = lambda ci: slice(ci * c, (ci + 1) * c)
            loaded = {ci: gate_sums(rows_of(ci), ci % SCRATCH_SLOTS) for ci in range(min(2, n_chunks))}
            ops = {0: operands(loaded.pop(0), 0)}
            for ci in range(n_chunks):
                heads = score_parts(ops.pop(ci))
                if ci + 2 < n_chunks:
                    loaded[ci + 2] = gate_sums(rows_of(ci + 2), (ci + 2) % SCRATCH_SLOTS)
                if ci + 1 < n_chunks:
                    ops[ci + 1] = operands(loaded.pop(ci + 1), (ci + 1) % SCRATCH_SLOTS)
                outputs(rows_of(ci), heads)
        else:
            def pair(p, carry):
                for half in range(2):
                    rows = pl.ds(pl.multiple_of((2 * p + half) * c, c), c)
                    outputs(rows, score_parts(operands(gate_sums(rows, half), half)))
                return carry

            lax.fori_loop(0, n_chunks // 2, pair, 0)

    worst = None
    for r0 in range(0, REC_ROWS, fast_unit):
        tot = jnp.sum(g_ref[0, r0:r0 + fast_unit, :], axis=0, keepdims=True)
        worst = tot if worst is None else jnp.minimum(worst, tot)
    mild = jnp.min(worst) >= -FAST_PATH_MAX_DECAY

    @pl.when(mild)
    def _():
        sweep(True)

    @pl.when(jnp.logical_not(mild))
    def _():
        sweep(False)


def _gated_linear_attention(q, k, v, g, og, norm_w, dk, n_heads, fast_unit, name):
    b, t, hk = q.shape
    hv = v.shape[-1]
    rows = REC_ROWS
    n_blocks = hk // LANES
    lay_r, lay_f = _UnitLayout(ROBUST_UNIT), _UnitLayout(fast_unit)
    consts = [norm_w,
              jnp.asarray(_gate_sum_matrix(lay_r), BF16), jnp.asarray(_level_table(lay_r, False)),
              jnp.asarray(_gate_sum_matrix(lay_f), BF16), jnp.asarray(_level_table(lay_f, True))]
    seq = lambda width: pl.BlockSpec((1, rows, width), lambda bi, ci: (bi, ci, 0))
    return pl.pallas_call(
        functools.partial(_gla_kernel, dk=dk, n_heads=n_heads, fast_unit=fast_unit),
        grid=(b, t // rows),
        in_specs=[seq(hk), seq(hk), seq(hv), seq(hk), seq(hv)] + [_const_spec(a.shape) for a in consts],
        out_specs=seq(hv),
        out_shape=jax.ShapeDtypeStruct((b, t, hv), BF16),
        scratch_shapes=[pltpu.VMEM((n_heads, LANES, LANES), F32),
                        pltpu.VMEM((SCRATCH_SLOTS * n_blocks, SUBLANES + CHUNK, LANES), F32),
                        pltpu.VMEM((SCRATCH_SLOTS * n_blocks, max(lay_r.packed_rows, lay_f.packed_rows), LANES), F32)],
        compiler_params=_params(("parallel", "arbitrary")),
        name=name,
    )(q, k, v, g, og, *consts)


def _merge_kernel(x_ref, oa_ref, ob_ref, oc_ref, wt_ref, wbr_ref, wout_ref, lnw_ref, lnb_ref, o_ref):
    for r0 in range(0, x_ref.shape[0], SUB_ROWS):
        rows = slice(r0, r0 + SUB_ROWS)
        x = x_ref[rows, :]
        xb = x.astype(BF16)
        mixed = None
        for n, br_ref in enumerate((oa_ref, ob_ref, oc_ref)):
            gate = _sigmoid(_dot(xb, wt_ref[:, _T_MG + n * D_MODEL:_T_MG + (n + 1) * D_MODEL]))
            y = gate * _dot(br_ref[rows, :], wbr_ref[n])
            mixed = y if mixed is None else mixed + y
        mix = _dot(mixed.astype(BF16), wout_ref[...])
        o_ref[rows, :] = _layer_norm(ALPHA * x + mix, lnw_ref[...], lnb_ref[...])


def _merge(x2d, oa, ob, oc, wt, wbr, wout, lnw, lnb):
    n = x2d.shape[0]
    tm = min(ROW_TILE, n)
    row = lambda width: pl.BlockSpec((tm, width), lambda i: (i, 0))
    return pl.pallas_call(
        _merge_kernel,
        grid=(n // tm,),
        in_specs=[row(D_MODEL), row(BRANCH_WIDTH), row(BRANCH_WIDTH), row(BRANCH_WIDTH),
                  _const_spec(wt.shape), _const_spec(wbr.shape), _const_spec(wout.shape),
                  _const_spec(lnw.shape), _const_spec(lnb.shape)],
        out_specs=row(D_MODEL),
        out_shape=jax.ShapeDtypeStruct((n, D_MODEL), F32),
        compiler_params=_params(("parallel",)),
        name="merge",
    )(x2d, oa, ob, oc, wt, wbr, wout, lnw, lnb)


def _mlp_kernel(x_ref, wup_ref, wdn_ref, lnw_ref, lnb_ref, o_ref):
    for r0 in range(0, x_ref.shape[0], SUB_ROWS):
        rows = slice(r0, r0 + SUB_ROWS)
        x = x_ref[rows, :]
        xb = x.astype(BF16)
        acc = None
        for f0 in range(0, D_FF, D_MODEL):
            hmid = jnp.maximum(_dot(xb, wup_ref[:, f0:f0 + D_MODEL]), 0.0)
            part = _dot((hmid * hmid).astype(BF16), wdn_ref[f0:f0 + D_MODEL, :])
            acc = part if acc is None else acc + part
        o_ref[rows, :] = _layer_norm(ALPHA * x + acc, lnw_ref[...], lnb_ref[...])


def _mlp(x2d, wup, wdn, lnw, lnb):
    n = x2d.shape[0]
    tm = min(ROW_TILE, n)
    row = pl.BlockSpec((tm, D_MODEL), lambda i: (i, 0))
    return pl.pallas_call(
        _mlp_kernel,
        grid=(n // tm,),
        in_specs=[row, _const_spec(wup.shape), _const_spec(wdn.shape),
                  _const_spec(lnw.shape), _const_spec(lnb.shape)],
        out_specs=row,
        out_shape=jax.ShapeDtypeStruct((n, D_MODEL), F32),
        compiler_params=_params(("parallel",)),
        name="mlp",
    )(x2d, wup, wdn, lnw, lnb)


def _layer(x2d, b, t, layer, w_in, gla_w_gate, gla_b_gate, diff_lambda, diff_norm_w, gla_norm_w,
           hgrn_norm_w, hgrn_lb, w_branch, w_out, ln1_w, ln1_b, w_up, w_down, ln2_w, ln2_b):
    n = b * t
    w = w_in[layer]
    wab = w[:, :_B_R].astype(BF16)
    wr = jnp.pad(w[:, _B_R:_B_G], ((0, 0), (0, LANES - GLA_GATE_RANK))).astype(BF16)
    wt = w[:, _B_G:].astype(BF16)
    gw = jnp.pad(gla_w_gate[layer], ((0, LANES - GLA_GATE_RANK), (0, 0)))
    gb = gla_b_gate[layer][None, :]

    (qa, ka, va, qg, kg, vg, lg, sgo, qh, lf, kh, ih, sho) = _in_proj(
        x2d, wab, wr, wt, gw, gb, hgrn_lb, layer)
    seq = lambda a: a.reshape(b, t, a.shape[-1])

    oa = _diff_attn(seq(qa), seq(ka), seq(va), diff_lambda[layer], diff_norm_w[layer][None, :], layer)
    ob = _gated_linear_attention(seq(qg), seq(kg), seq(vg), seq(lg), seq(sgo),
                                 gla_norm_w[layer][None, :], GLA_DK, GLA_HEADS, GLA_FAST_UNIT, "gla")
    oc = _gated_linear_attention(seq(qh), seq(kh), seq(ih), seq(lf), seq(sho),
                                 hgrn_norm_w[layer][None, :], HGRN_EXPAND, HGRN_HEADS, HGRN_FAST_UNIT, "hgrn")

    flat = lambda a: a.reshape(n, a.shape[-1])
    x1 = _merge(x2d, flat(oa), flat(ob), flat(oc), wt, w_branch[layer], w_out[layer],
                ln1_w[layer][None, :], ln1_b[layer][None, :])
    return _mlp(x1, w_up[layer], w_down[layer], ln2_w[layer][None, :], ln2_b[layer][None, :])


def kernel(x, w_in, gla_w_gate, gla_b_gate, diff_lambda, diff_norm_w, gla_norm_w, hgrn_norm_w, hgrn_lb,
           w_branch, w_out, ln1_w, ln1_b, w_up, w_down, ln2_w, ln2_b):
    b, t, d = x.shape
    assert d == D_MODEL and w_in.shape == (DEPTH, D_MODEL, IN_WIDTH)
    assert t % REC_ROWS == 0 and t % ATTN_TILE == 0 and (b * t) % ROW_TILE == 0 and ROW_TILE % SUB_ROWS == 0
    x2d = x.reshape(b * t, d)
    w_branch, w_out, w_up, w_down = (a.astype(BF16) for a in (w_branch, w_out, w_up, w_down))
    for layer in range(DEPTH):
        x2d = _layer(x2d, b, t, layer, w_in, gla_w_gate, gla_b_gate, diff_lambda, diff_norm_w,
                     gla_norm_w, hgrn_norm_w, hgrn_lb, w_branch, w_out, ln1_w, ln1_b,
                     w_up, w_down, ln2_w, ln2_b)
    return x2d.reshape(b, t, d)
```

```python
import functools
import math

import numpy as np
import jax
import jax.numpy as jnp
from jax import lax
from jax.experimental import pallas as pl
from jax.experimental.pallas import tpu as pltpu

D_MODEL = 1024
DEPTH = 2
DIFF_HEADS = 4
DIFF_QK_DIM = 64
DIFF_V_DIM = 128
GLA_HEADS = 4
GLA_DK = 64
GLA_DV = 128
GLA_GATE_RANK = 16
GLA_GATE_NORM = 16.0
HGRN_HEADS = 4
HGRN_EXPAND = 128
HGRN_DV = 128
N_BRANCHES = 3
BRANCH_WIDTH = 512
D_FF = 4 * D_MODEL
ALPHA = (2 * DEPTH) ** 0.25
LN_EPS = 1e-5
MASK_VALUE = -1e30
LB_FLOOR = 1e-30

LANES = 128
SUBLANES = 8
VMEM_LIMIT_BYTES = 56 * 1024 * 1024

ROW_TILE = 1024
SUB_ROWS = 256
ATTN_TILE = 512
CHUNK = 128
REC_ROWS = 2048
FAST_PATH_MAX_DECAY = 60.0
SCRATCH_SLOTS = 3
GLA_FAST_UNIT = 128
HGRN_FAST_UNIT = 32

F32 = jnp.float32
BF16 = jnp.bfloat16

_A_Q, _A_K, _A_V = 0, 512, 1024
_B_Q, _B_K, _B_V, _B_R, _B_G = 1536, 1792, 2048, 2560, 2576
_C_Q, _C_F, _C_I, _C_G = 3088, 3600, 4112, 4624
_MG = 5136
IN_WIDTH = _MG + N_BRANCHES * D_MODEL
_T_BG, _T_CQ, _T_CF, _T_CI, _T_CG, _T_MG = (o - _B_G for o in (_B_G, _C_Q, _C_F, _C_I, _C_G, _MG))


def _dot(a, b):
    return jnp.dot(a, b, preferred_element_type=F32)


def _dot_nt(a, b):
    return lax.dot_general(a, b, (((1,), (1,)), ((), ())), preferred_element_type=F32)


def _dot_tn(a, b):
    return lax.dot_general(a, b, (((0,), (0,)), ((), ())), preferred_element_type=F32)


def _split_bf16(a):
    hi = a.astype(BF16)
    r = a - hi.astype(F32)
    mid = r.astype(BF16)
    lo = (r - mid.astype(F32)).astype(BF16)
    return hi, mid, lo


def _log_sigmoid(u):
    return jnp.minimum(u, 0.0) - jnp.log1p(jnp.exp(-jnp.abs(u)))


def _sigmoid(u):
    return 1.0 / (1.0 + jnp.exp(-u))


def _layer_norm(r, w, b):
    mu = jnp.mean(r, axis=-1, keepdims=True)
    c = r - mu
    var = jnp.mean(c * c, axis=-1, keepdims=True)
    return c * lax.rsqrt(var + LN_EPS) * w + b


def _rms_norm(o, w):
    return o * lax.rsqrt(jnp.mean(o * o, axis=-1, keepdims=True) + LN_EPS) * w


def _const_spec(shape):
    nd = len(shape)
    return pl.BlockSpec(shape, lambda *_: (0,) * nd, pipeline_mode=pl.Buffered(1))


def _params(semantics):
    return pltpu.CompilerParams(dimension_semantics=semantics, vmem_limit_bytes=VMEM_LIMIT_BYTES)


def _in_proj_kernel(x_ref, wab_ref, wr_ref, wt_ref, gw_ref, gb_ref, lb_ref,
                    qa_ref, ka_ref, va_ref, qg_ref, kg_ref, vg_ref, lg_ref, sgo_ref,
                    qh_ref, lf_ref, kh_ref, ih_ref, sho_ref, *, layer):
    lb_rows = [lb_ref[d:d + 1, :] for d in range(DEPTH)]
    mx = functools.reduce(jnp.maximum, lb_rows)
    es = [jnp.exp(r - mx) for r in lb_rows]
    tot = functools.reduce(jnp.add, es)
    soft = [e / tot for e in es]
    lb = functools.reduce(jnp.add, soft[:layer + 1]) - soft[0]
    lb_floored = jnp.maximum(lb, LB_FLOOR)
    one_minus_lb = 1.0 - lb
    w_hi, w_mid, _ = _split_bf16(gw_ref[...])

    for r0 in range(0, x_ref.shape[0], SUB_ROWS):
        rows = slice(r0, r0 + SUB_ROWS)
        xb = x_ref[rows, :].astype(BF16)

        def proj(w_ref, lo, width):
            return _dot(xb, w_ref[:, lo:lo + width])

        f = lb_floored + one_minus_lb * _sigmoid(proj(wt_ref, _T_CF, 512))
        lf_ref[rows, :] = jnp.log(f)
        kh_ref[rows, :] = (1.0 - f).astype(BF16)
        qa_ref[rows, :] = (proj(wab_ref, _A_Q, 512) * DIFF_QK_DIM ** -0.5).astype(BF16)
        ka_ref[rows, :] = proj(wab_ref, _A_K, 512).astype(BF16)
        gout = proj(wt_ref, _T_BG, 512)
        sgo_ref[rows, :] = (gout * _sigmoid(gout)).astype(BF16)
        va_ref[rows, :] = proj(wab_ref, _A_V, 512).astype(BF16)
        qh_ref[rows, :] = proj(wt_ref, _T_CQ, 512).astype(BF16)
        hout = proj(wt_ref, _T_CG, 512)
        sho_ref[rows, :] = (hout * _sigmoid(hout)).astype(BF16)
        ih_ref[rows, :] = proj(wt_ref, _T_CI, 512).astype(BF16)
        vg_ref[rows, :] = proj(wab_ref, _B_V, 512).astype(BF16)
        glr = proj(wr_ref, 0, LANES)
        a_hi, a_mid, _ = _split_bf16(glr)
        u = _dot(a_hi, w_hi) + (_dot(a_hi, w_mid) + _dot(a_mid, w_hi)) + gb_ref[...]
        lg_ref[rows, :] = _log_sigmoid(u) * (1.0 / GLA_GATE_NORM)
        qg_ref[rows, :] = (proj(wab_ref, _B_Q, 256) * GLA_DK ** -0.5).astype(BF16)
        kg_ref[rows, :] = proj(wab_ref, _B_K, 256).astype(BF16)


def _in_proj(x2d, wab, wr, wt, gw, gb, lb, layer):
    n = x2d.shape[0]
    tm = min(ROW_TILE, n)
    row = lambda width: pl.BlockSpec((tm, width), lambda i: (i, 0))
    out_widths = [(512, BF16), (512, BF16), (512, BF16),
                  (256, BF16), (256, BF16), (512, BF16), (256, F32), (512, BF16),
                  (512, BF16), (512, F32), (512, BF16), (512, BF16), (512, BF16)]
    return pl.pallas_call(
        functools.partial(_in_proj_kernel, layer=layer),
        grid=(n // tm,),
        in_specs=[row(D_MODEL), _const_spec(wab.shape), _const_spec(wr.shape),
                  _const_spec((D_MODEL, _T_MG)),
                  _const_spec(gw.shape), _const_spec(gb.shape), _const_spec(lb.shape)],
        out_specs=[row(w) for w, _ in out_widths],
        out_shape=[jax.ShapeDtypeStruct((n, w), dt) for w, dt in out_widths],
        compiler_params=_params(("parallel",)),
        name="in_proj",
    )(x2d, wab, wr, wt, gw, gb, lb)


def _diff_attn_kernel(q_ref, k_ref, v_ref, lam_ref, nw_ref, o_ref, m_scr, l_scr, acc_scr, *, layer):
    t = ATTN_TILE
    qi = pl.program_id(1)

    lane = lax.broadcasted_iota(jnp.int32, (t, LANES), 1)
    qs = []
    for h in range(DIFF_HEADS):
        q = q_ref[0, :, h * LANES:(h + 1) * LANES]
        zero = jnp.zeros_like(q)
        qs.append(jnp.concatenate([jnp.where(lane < DIFF_QK_DIM, q, zero),
                                   jnp.where(lane >= DIFF_QK_DIM, q, zero)], axis=0))

    m_scr[...] = jnp.full(m_scr.shape, -jnp.inf, F32)
    l_scr[...] = jnp.zeros(l_scr.shape, F32)
    acc_scr[...] = jnp.zeros(acc_scr.shape, F32)
    col = lax.broadcasted_iota(jnp.int32, (1, t), 1)
    ones = jnp.ones((t, LANES), BF16)

    def block(j, masked):
        off = pl.multiple_of(j * t, t)
        rel = ((j - qi) * t + col).astype(F32)
        if masked:
            r = lax.broadcasted_iota(jnp.int32, (t, t), 0)
            c = lax.broadcasted_iota(jnp.int32, (t, t), 1)
            keep = jnp.concatenate([r >= c, r >= c], axis=0)

        def scores(h):
            return _dot_nt(qs[h], k_ref[0, pl.ds(off, t), h * LANES:(h + 1) * LANES])

        s_next = scores(0)
        for h in range(DIFF_HEADS):
            s = s_next
            if h + 1 < DIFF_HEADS:
                s_next = scores(h + 1)
            s = s + 2.0 ** (-8.0 * (h + 1) / DIFF_HEADS) * rel
            if masked:
                s = jnp.where(keep, s, MASK_VALUE)
            m_prev = m_scr[h]
            m_next = jnp.maximum(m_prev, jnp.max(s, axis=-1, keepdims=True))
            corr = jnp.exp(m_prev - m_next)
            p = jnp.concatenate([jnp.exp(s[:, b0:b0 + LANES] - m_next) for b0 in range(0, t, LANES)],
                                axis=1).astype(BF16)
            vb = v_ref[0, pl.ds(off, t), h * LANES:(h + 1) * LANES]
            pv = _dot(p, jnp.concatenate([vb, ones], axis=1))
            acc_scr[h] = acc_scr[h] * corr + pv[:, :LANES]
            l_scr[h] = l_scr[h] * corr + pv[:, LANES:]
            m_scr[h] = m_next

    def body(j, carry):
        block(j, False)
        return carry

    lax.fori_loop(0, qi, body, 0)
    block(qi, True)

    lp = lam_ref[...]
    lam_init = 0.8 - 0.6 * math.exp(-0.3 * layer)
    lam = (jnp.exp(jnp.sum(lp[0:1] * lp[1:2], axis=-1, keepdims=True))
           - jnp.exp(jnp.sum(lp[2:3] * lp[3:4], axis=-1, keepdims=True)) + lam_init)
    nw = nw_ref[...]
    for h in range(DIFF_HEADS):
        o12 = acc_scr[h] / l_scr[h]
        o = o12[:t] - lam * o12[t:]
        o_ref[0, :, h * LANES:(h + 1) * LANES] = (_rms_norm(o, nw) * (1.0 - lam_init)).astype(o_ref.dtype)


def _diff_attn(qa, ka, va, lam_p, norm_w, layer):
    b, t, width = qa.shape
    tq = ATTN_TILE
    scr = pltpu.VMEM((DIFF_HEADS, 2 * tq, LANES), F32)
    return pl.pallas_call(
        functools.partial(_diff_attn_kernel, layer=layer),
        grid=(b, t // tq),
        in_specs=[pl.BlockSpec((1, tq, width), lambda bi, qi: (bi, qi, 0)),
                  pl.BlockSpec((1, t, width), lambda bi, qi: (bi, 0, 0)),
                  pl.BlockSpec((1, t, width), lambda bi, qi: (bi, 0, 0)),
                  _const_spec(lam_p.shape), _const_spec(norm_w.shape)],
        out_specs=pl.BlockSpec((1, tq, width), lambda bi, qi: (bi, qi, 0)),
        out_shape=jax.ShapeDtypeStruct((b, t, width), BF16),
        scratch_shapes=[scr, scr, scr],
        compiler_params=_params(("parallel", "arbitrary")),
        name="diff_attn",
    )(qa, ka, va, lam_p, norm_w)


_N_LEVELS = int(math.log2(CHUNK))
ROBUST_UNIT = SUBLANES


class _UnitLayout:
    def __init__(self, unit):
        self.unit = unit
        self.level0 = int(math.log2(unit))
        self.n_units = CHUNK // unit
        self.row_p = max(_N_LEVELS - self.level0 - 1, 0) * self.n_units
        self.row_q = self.row_p + self.n_units
        self.row_last = self.row_q + self.n_units
        self.packed_rows = -(-(self.row_last + 1) // 16) * 16

    def factor_row(self, li):
        return (li - self.level0 - 1) * self.n_units


def _level_table(lay, fast):
    i = np.arange(CHUNK)[:, None]
    j = np.arange(CHUNK)[None, :]
    lvl = np.where(i > j, np.floor(np.log2(np.maximum(np.bitwise_xor(i, j), 1))), -1.0)
    if fast:
        lvl = np.where(i > j, np.maximum(lvl, lay.level0 - 1), -1.0)
    return lvl.astype(np.int32)


def _gate_sum_matrix(lay):
    c, nu, u = CHUNK, lay.n_units, lay.unit
    m = np.zeros((c + lay.packed_rows, c), np.float32)
    m[:c] = np.tril(np.ones((c, c), np.float32))
    for li in range(lay.level0 + 1, _N_LEVELS):
        half = (1 << li) // u
        for unit in range(nu):
            first = (unit // (2 * half)) * 2 * half
            boundary = u * (first + half)
            if unit - first >= half:
                m[c + lay.factor_row(li) + unit, boundary:u * unit] = 1.0
            else:
                m[c + lay.factor_row(li) + unit, u * unit + u:boundary] = 1.0
    for unit in range(nu):
        m[c + lay.row_p + unit, :u * unit] = 1.0
        m[c + lay.row_q + unit, u * unit + u:] = 1.0
    m[c + lay.row_last, :] = 1.0
    return np.concatenate([m, m], axis=1)


def _unit_rows(ref, blk, first_row, stride, lay):
    parts = [jnp.broadcast_to(ref[blk, pl.ds(first_row + stride * unit, 1), :], (lay.unit, LANES))
             for unit in range(lay.n_units)]
    return parts[0] if len(parts) == 1 else jnp.concatenate(parts, axis=0)


def _gla_kernel(q_ref, k_ref, v_ref, g_ref, og_ref, nw_ref, gsum_r_ref, lvl_r_ref, gsum_f_ref, lvl_f_ref,
                o_ref, st_scr, cum_scr, fac_scr, *, dk, n_heads, fast_unit):
    heads_per_block = LANES // dk
    n_blocks = n_heads // heads_per_block
    c = CHUNK
    sl = SUBLANES
    pad = SUBLANES

    @pl.when(pl.program_id(1) == 0)
    def _():
        st_scr[...] = jnp.zeros(st_scr.shape, F32)

    cum_scr[:, 0:pad, :] = jnp.zeros((SCRATCH_SLOTS * n_blocks, pad, LANES), F32)
    nw = nw_ref[...]
    lane = lax.broadcasted_iota(jnp.int32, (c, LANES), 1)
    sub = lax.broadcasted_iota(jnp.int32, (sl, LANES), 0)

    def sweep(fast):
        lay = _UnitLayout(fast_unit if fast else ROBUST_UNIT)
        gsum = (gsum_f_ref if fast else gsum_r_ref)[...]
        lvl = (lvl_f_ref if fast else lvl_r_ref)[...]

        def gate_sums(rows, slot):
            loaded = []
            for blk in range(n_blocks):
                lanes = slice(blk * LANES, (blk + 1) * LANES)
                g = g_ref[0, rows, lanes]
                kb = k_ref[0, rows, lanes]
                qf = q_ref[0, rows, lanes].astype(F32)
                g_hi, g_mid, _ = _split_bf16(g)
                sums = _dot(gsum, jnp.concatenate([g_hi, g_mid], axis=0))
                cum_scr[slot * n_blocks + blk, pad:pad + c, :] = sums[:c]
                fac_scr[slot * n_blocks + blk, 0:lay.packed_rows, :] = jnp.exp(sums[c:])
                loaded.append((g, kb, qf, sums[:c]))
            return loaded

        def operands(loaded, slot):
            out = []
            for lane_blk, (g, kb, qf, cum) in enumerate(loaded):
                blk = slot * n_blocks + lane_blk
                kf = kb.astype(F32)
                before = cum - _unit_rows(cum_scr, blk, pad - 1, lay.unit, lay)
                q_u = qf * jnp.exp(before)
                k_u = kf * jnp.exp(_unit_rows(cum_scr, blk, pad + lay.unit - 1, lay.unit, lay) - cum)
                terms = []
                if fast:
                    terms.append((lay.level0 - 1, q_u.astype(BF16), (kf * jnp.exp(-before)).astype(BF16)))
                else:
                    terms.append((0, (qf * jnp.exp(g)).astype(BF16), kb))
                    m2 = jnp.concatenate(
                        [jnp.where(sub < 4,
                                   jnp.broadcast_to(cum_scr[blk, pl.ds(pad + sl * t + 1, 1), :], (sl, LANES)),
                                   jnp.broadcast_to(cum_scr[blk, pl.ds(pad + sl * t + 5, 1), :], (sl, LANES)))
                         for t in range(lay.n_units)], axis=0)
                    for li, m in ((1, m2), (2, _unit_rows(cum_scr, blk, pad + 3, sl, lay))):
                        e = jnp.exp(-jnp.abs(cum - m))
                        terms.append((li, (qf * e).astype(BF16), (kf * e).astype(BF16)))
                if lay.level0 < _N_LEVELS:
                    terms.append((lay.level0, q_u.astype(BF16), k_u.astype(BF16)))
                for li in range(lay.level0 + 1, _N_LEVELS):
                    f = _unit_rows(fac_scr, blk, lay.factor_row(li), 1, lay)
                    terms.append((li, (q_u * f).astype(BF16), (k_u * f).astype(BF16)))
                q_in = (q_u * _unit_rows(fac_scr, blk, lay.row_p, 1, lay)).astype(BF16)
                k_out = (k_u * _unit_rows(fac_scr, blk, lay.row_q, 1, lay)).astype(BF16)
                decay = fac_scr[blk, pl.ds(lay.row_last, 1), :]
                qk = qf * kf
                out.append((terms, q_in, k_out, decay, qk))
            return out

        def score_parts(ops):
            heads = []
            for blk, (terms, q_in, k_out, decay, qk) in enumerate(ops):
                for hb in range(heads_per_block):
                    in_head = (lane >= hb * dk) & (lane < (hb + 1) * dk)
                    sel = (lambda a: a) if heads_per_block == 1 else (
                        lambda a, in_head=in_head: jnp.where(in_head, a, jnp.zeros_like(a)))
                    parts = [(li, _dot_nt(sel(q_l), k_l)) for li, q_l, k_l in terms]
                    heads.append((blk * heads_per_block + hb, sel, parts, q_in, k_out, decay, qk))
            return heads

        def outputs(rows, heads):
            outs = []
            for h, sel, parts, q_in, k_out, decay, qk in heads:
                vh = v_ref[0, rows, h * LANES:(h + 1) * LANES]
                scores = jnp.zeros((c, c), F32)
                for li, part in parts:
                    scores = jnp.where(lvl == li, part, scores)
                st = st_scr[h]
                o = _dot(scores.astype(BF16), vh) + _dot_nt(sel(q_in), st.astype(BF16))
                o = o + jnp.sum(sel(qk), axis=-1, keepdims=True) * vh.astype(F32)
                outs.append((h, vh, st, o, k_out, decay))

            for h, vh, st, o, k_out, decay in outs:
                hl = slice(h * LANES, (h + 1) * LANES)
                st_scr[h] = st * decay + _dot_tn(vh, k_out)
                og = og_ref[0, rows, hl].astype(F32)
                o_ref[0, rows, hl] = (_rms_norm(o, nw) * og).astype(o_ref.dtype)

        n_chunks = REC_ROWS // c
        if fast:
            rows_of = lambda ci: slice(ci * c, (ci + 1) * c)
            loaded = {ci: gate_sums(rows_of(ci), ci % SCRATCH_SLOTS) for ci in range(min(2, n_chunks))}
            ops = {0: operands(loaded.pop(0), 0)}
            for ci in range(n_chunks):
                heads = score_parts(ops.pop(ci))
                if ci + 2 < n_chunks:
                    loaded[ci + 2] = gate_sums(rows_of(ci + 2), (ci + 2) % SCRATCH_SLOTS)
                if ci + 1 < n_chunks:
                    ops[ci + 1] = operands(loaded.pop(ci + 1), (ci + 1) % SCRATCH_SLOTS)
                outputs(rows_of(ci), heads)
        else:
            def pair(p, carry):
                for half in range(2):
                    rows = pl.ds(pl.multiple_of((2 * p + half) * c, c), c)
                    outputs(rows, score_parts(operands(gate_sums(rows, half), half)))
                return carry

            lax.fori_loop(0, n_chunks // 2, pair, 0)

    worst = None
    for r0 in range(0, REC_ROWS, fast_unit):
        tot = jnp.sum(g_ref[0, r0:r0 + fast_unit, :], axis=0, keepdims=True)
        worst = tot if worst is None else jnp.minimum(worst, tot)
    mild = jnp.min(worst) >= -FAST_PATH_MAX_DECAY

    @pl.when(mild)
    def _():
        sweep(True)

    @pl.when(jnp.logical_not(mild))
    def _():
        sweep(False)


def _gated_linear_attention(q, k, v, g, og, norm_w, dk, n_heads, fast_unit, name):
    b, t, hk = q.shape
    hv = v.shape[-1]
    rows = REC_ROWS
    n_blocks = hk // LANES
    lay_r, lay_f = _UnitLayout(ROBUST_UNIT), _UnitLayout(fast_unit)
    consts = [norm_w,
              jnp.asarray(_gate_sum_matrix(lay_r), BF16), jnp.asarray(_level_table(lay_r, False)),
              jnp.asarray(_gate_sum_matrix(lay_f), BF16), jnp.asarray(_level_table(lay_f, True))]
    seq = lambda width: pl.BlockSpec((1, rows, width), lambda bi, ci: (bi, ci, 0))
    return pl.pallas_call(
        functools.partial(_gla_kernel, dk=dk, n_heads=n_heads, fast_unit=fast_unit),
        grid=(b, t // rows),
        in_specs=[seq(hk), seq(hk), seq(hv), seq(hk), seq(hv)] + [_const_spec(a.shape) for a in consts],
        out_specs=seq(hv),
        out_shape=jax.ShapeDtypeStruct((b, t, hv), BF16),
        scratch_shapes=[pltpu.VMEM((n_heads, LANES, LANES), F32),
                        pltpu.VMEM((SCRATCH_SLOTS * n_blocks, SUBLANES + CHUNK, LANES), F32),
                        pltpu.VMEM((SCRATCH_SLOTS * n_blocks, max(lay_r.packed_rows, lay_f.packed_rows), LANES), F32)],
        compiler_params=_params(("parallel", "arbitrary")),
        name=name,
    )(q, k, v, g, og, *consts)


def _merge_kernel(x_ref, oa_ref, ob_ref, oc_ref, wt_ref, wbr_ref, wout_ref, lnw_ref, lnb_ref, o_ref):
    for r0 in range(0, x_ref.shape[0], SUB_ROWS):
        rows = slice(r0, r0 + SUB_ROWS)
        x = x_ref[rows, :]
        xb = x.astype(BF16)
        mixed = None
        for n, br_ref in enumerate((oa_ref, ob_ref, oc_ref)):
            gate = _sigmoid(_dot(xb, wt_ref[:, _T_MG + n * D_MODEL:_T_MG + (n + 1) * D_MODEL]))
            y = gate * _dot(br_ref[rows, :], wbr_ref[n])
            mixed = y if mixed is None else mixed + y
        mix = _dot(mixed.astype(BF16), wout_ref[...])
        o_ref[rows, :] = _layer_norm(ALPHA * x + mix, lnw_ref[...], lnb_ref[...])


def _merge(x2d, oa, ob, oc, wt, wbr, wout, lnw, lnb):
    n = x2d.shape[0]
    tm = min(ROW_TILE, n)
    row = lambda width: pl.BlockSpec((tm, width), lambda i: (i, 0))
    return pl.pallas_call(
        _merge_kernel,
        grid=(n // tm,),
        in_specs=[row(D_MODEL), row(BRANCH_WIDTH), row(BRANCH_WIDTH), row(BRANCH_WIDTH),
                  _const_spec(wt.shape), _const_spec(wbr.shape), _const_spec(wout.shape),
                  _const_spec(lnw.shape), _const_spec(lnb.shape)],
        out_specs=row(D_MODEL),
        out_shape=jax.ShapeDtypeStruct((n, D_MODEL), F32),
        compiler_params=_params(("parallel",)),
        name="merge",
    )(x2d, oa, ob, oc, wt, wbr, wout, lnw, lnb)


def _mlp_kernel(x_ref, wup_ref, wdn_ref, lnw_ref, lnb_ref, o_ref):
    for r0 in range(0, x_ref.shape[0], SUB_ROWS):
        rows = slice(r0, r0 + SUB_ROWS)
        x = x_ref[rows, :]
        xb = x.astype(BF16)
        acc = None
        for f0 in range(0, D_FF, D_MODEL):
            hmid = jnp.maximum(_dot(xb, wup_ref[:, f0:f0 + D_MODEL]), 0.0)
            part = _dot((hmid * hmid).astype(BF16), wdn_ref[f0:f0 + D_MODEL, :])
            acc = part if acc is None else acc + part
        o_ref[rows, :] = _layer_norm(ALPHA * x + acc, lnw_ref[...], lnb_ref[...])


def _mlp(x2d, wup, wdn, lnw, lnb):
    n = x2d.shape[0]
    tm = min(ROW_TILE, n)
    row = pl.BlockSpec((tm, D_MODEL), lambda i: (i, 0))
    return pl.pallas_call(
        _mlp_kernel,
        grid=(n // tm,),
        in_specs=[row, _const_spec(wup.shape), _const_spec(wdn.shape),
                  _const_spec(lnw.shape), _const_spec(lnb.shape)],
        out_specs=row,
        out_shape=jax.ShapeDtypeStruct((n, D_MODEL), F32),
        compiler_params=_params(("parallel",)),
        name="mlp",
    )(x2d, wup, wdn, lnw, lnb)


def _layer(x2d, b, t, layer, w_in, gla_w_gate, gla_b_gate, diff_lambda, diff_norm_w, gla_norm_w,
           hgrn_norm_w, hgrn_lb, w_branch, w_out, ln1_w, ln1_b, w_up, w_down, ln2_w, ln2_b):
    n = b * t
    w = w_in[layer]
    wab = w[:, :_B_R].astype(BF16)
    wr = jnp.pad(w[:, _B_R:_B_G], ((0, 0), (0, LANES - GLA_GATE_RANK))).astype(BF16)
    wt = w[:, _B_G:].astype(BF16)
    gw = jnp.pad(gla_w_gate[layer], ((0, LANES - GLA_GATE_RANK), (0, 0)))
    gb = gla_b_gate[layer][None, :]

    (qa, ka, va, qg, kg, vg, lg, sgo, qh, lf, kh, ih, sho) = _in_proj(
        x2d, wab, wr, wt, gw, gb, hgrn_lb, layer)
    seq = lambda a: a.reshape(b, t, a.shape[-1])

    oa = _diff_attn(seq(qa), seq(ka), seq(va), diff_lambda[layer], diff_norm_w[layer][None, :], layer)
    ob = _gated_linear_attention(seq(qg), seq(kg), seq(vg), seq(lg), seq(sgo),
                                 gla_norm_w[layer][None, :], GLA_DK, GLA_HEADS, GLA_FAST_UNIT, "gla")
    oc = _gated_linear_attention(seq(qh), seq(kh), seq(ih), seq(lf), seq(sho),
                                 hgrn_norm_w[layer][None, :], HGRN_EXPAND, HGRN_HEADS, HGRN_FAST_UNIT, "hgrn")

    flat = lambda a: a.reshape(n, a.shape[-1])
    x1 = _merge(x2d, flat(oa), flat(ob), flat(oc), wt, w_branch[layer], w_out[layer],
                ln1_w[layer][None, :], ln1_b[layer][None, :])
    return _mlp(x1, w_up[layer], w_down[layer], ln2_w[layer][None, :], ln2_b[layer][None, :])


def kernel(x, w_in, gla_w_gate, gla_b_gate, diff_lambda, diff_norm_w, gla_norm_w, hgrn_norm_w, hgrn_lb,
           w_branch, w_out, ln1_w, ln1_b, w_up, w_down, ln2_w, ln2_b):
    b, t, d = x.shape
    assert d == D_MODEL and w_in.shape == (DEPTH, D_MODEL, IN_WIDTH)
    assert t % REC_ROWS == 0 and t % ATTN_TILE == 0 and (b * t) % ROW_TILE == 0 and ROW_TILE % SUB_ROWS == 0
    x2d = x.reshape(b * t, d)
    w_branch, w_out, w_up, w_down = (a.astype(BF16) for a in (w_branch, w_out, w_up, w_down))
    for layer in range(DEPTH):
        x2d = _layer(x2d, b, t, layer, w_in, gla_w_gate, gla_b_gate, diff_lambda, diff_norm_w,
                     gla_norm_w, hgrn_norm_w, hgrn_lb, w_branch, w_out, ln1_w, ln1_b,
                     w_up, w_down, ln2_w, ln2_b)
    return x2d.reshape(b, t, d)
```
